```python
import jax, jax.numpy as jnp
from jax import lax
import numpy as np

D_MODEL = 1024
BATCH = 4
SEQ = 8192
DEPTH = 2
DEC_BATCH = 128
DEC_SEQ = 1
PAST_LEN = 16384
PAGE_SIZE = 128

HEAD_DIM = 64
NSA_HEADS = 8
NSA_KV = 1
CMP_BLOCK = 64
SEL_BLOCK = 64
SEL_TOP_K = 16
WINDOW = 512
MLA_HEADS = 8
MLA_Q_RANK = 256
MLA_KV_RANK = 128
MLA_NOPE = 64
MLA_ROPE = 32
MLA_V = 64
ROPE_THETA = 10000.0
SB_HEADS = 8
SB_KV = 2
N_BRANCH = 3
BRANCH_WIDTH = 512
D_FF = ((8 * D_MODEL + 3 * 256 - 1) // (3 * 256)) * 256
Q_BLOCK = 128
EPS = 1e-6
NEG = -1e30
FORCED_SCORE = 1e4
NSA_SCALE = HEAD_DIM ** -0.5
MLA_SCALE = (MLA_NOPE + MLA_ROPE) ** -0.5
IN_SIZES = (NSA_HEADS * HEAD_DIM,
            3 * 2 * NSA_KV * HEAD_DIM,
            3 * NSA_HEADS,
            MLA_Q_RANK,
            MLA_KV_RANK,
            MLA_ROPE,
            SB_HEADS * HEAD_DIM,
            2 * SB_KV * HEAD_DIM,
            N_BRANCH * D_MODEL)
D_IN = sum(IN_SIZES)

kernel_name = 'hybrid_nsa_mla_stickbreak_decode_step'


def rms_norm(x, g):
    xf = x.astype(jnp.float32)
    y = xf * lax.rsqrt(jnp.mean(xf * xf, axis=-1, keepdims=True) + EPS)
    return (y * g.astype(jnp.float32)).astype(x.dtype)


def rope(x, pos):
    half = x.shape[-1] // 2
    freq = ROPE_THETA ** (-jnp.arange(half, dtype=jnp.float32) / half)
    ang = pos.astype(jnp.float32)[:, None] * freq
    ang = ang.reshape(ang.shape[:1] + (1,) * (x.ndim - 3) + (half,))
    cos, sin = jnp.cos(ang), jnp.sin(ang)
    xf = x.astype(jnp.float32)
    x1, x2 = xf[..., :half], xf[..., half:]
    return jnp.concatenate([x1 * cos - x2 * sin, x1 * sin + x2 * cos], axis=-1).astype(x.dtype)


def alibi_slopes(n):
    return 2.0 ** (-8.0 * (jnp.arange(n, dtype=jnp.float32) + 1.0) / n)


def ada_modulation(c, w, b):
    mod = jax.nn.silu(c) @ w + b
    return jnp.split(mod[:, None, :], 6, axis=-1)


def modulate(x, g, shift, scale):
    return rms_norm(x, g) * (1.0 + scale) + shift


def swiglu(h, w_up, w_down):
    a, b = jnp.split(h @ w_up, 2, axis=-1)
    return (jax.nn.silu(a) * b) @ w_down


def split_in(z):
    B, T = z.shape[:2]
    offs = np.cumsum(IN_SIZES)[:-1].tolist()
    nq, nkv, ng, cq, ckv, kr, sq, skv, mg = jnp.split(z, offs, axis=-1)
    return (nq.reshape(B, T, NSA_HEADS, HEAD_DIM),
            nkv.reshape(B, T, 3, 2, NSA_KV, HEAD_DIM),
            jax.nn.sigmoid(ng).reshape(B, T, 3, NSA_HEADS),
            cq, ckv, kr,
            sq.reshape(B, T, SB_HEADS, HEAD_DIM),
            skv.reshape(B, T, 2, SB_KV, HEAD_DIM),
            jax.nn.sigmoid(mg).reshape(B, T, N_BRANCH, D_MODEL))


def map_query_blocks(fn, *xs):
    B, T = xs[0].shape[:2]
    nb = T // Q_BLOCK
    blocked = tuple(jnp.moveaxis(x.reshape((B, nb, Q_BLOCK) + x.shape[2:]), 1, 0) for x in xs)
    out = lax.map(lambda a: fn(a[0], *a[1:]), (jnp.arange(nb, dtype=jnp.int32),) + blocked)
    return jnp.moveaxis(out, 0, 1).reshape((B, T) + out.shape[3:])


def block_means(rows):
    B, L = rows.shape[:2]
    return rows.reshape((B, L // CMP_BLOCK, CMP_BLOCK) + rows.shape[2:]).mean(axis=2)


def take_rows(rows, kpos):
    b_ix = jnp.arange(rows.shape[0])[:, None, None, None, None]
    g_ix = jnp.arange(rows.shape[3])[None, None, :, None, None]
    return rows[b_ix, kpos, :, g_ix, :]


def take_paged_rows(cache, l, page_table, kpos):
    b_ix = jnp.arange(page_table.shape[0])[:, None, None, None, None]
    g_ix = jnp.arange(cache.shape[4])[None, None, :, None, None]
    page = page_table[b_ix, kpos // PAGE_SIZE]
    return cache[l, page, kpos % PAGE_SIZE, :, g_ix, :]


def nsa_attend(q, qpos, gates, cmp_k, cmp_v, fetch_sel, win_rows, win_pos, q_norm, k_norm):
    B, T = q.shape[:2]
    G = NSA_HEADS // NSA_KV
    nb = cmp_k.shape[1]
    slopes = alibi_slopes(NSA_HEADS).reshape(NSA_KV, G)
    qh = rms_norm(q, q_norm).reshape(B, T, NSA_KV, G, HEAD_DIM)
    kc = rms_norm(cmp_k, k_norm[0])
    cmp_end = (jnp.arange(nb, dtype=jnp.int32) + 1) * CMP_BLOCK - 1
    dist = (qpos[:, None] - cmp_end[None, :]).astype(jnp.float32)
    ok = dist >= 0
    s = jnp.einsum('btkgd,bnkd->btkgn', qh, kc).astype(jnp.float32) * NSA_SCALE
    s = s - slopes[:, :, None] * dist[:, None, None, :]
    s = jnp.where(ok[:, None, None, :], s, NEG)
    p_cmp = jax.nn.softmax(s, axis=-1) * ok[:, None, None, :]
    o_cmp = jnp.einsum('btkgn,bnkd->btkgd', p_cmp, cmp_v.astype(jnp.float32))
    cur = qpos // SEL_BLOCK
    j = jnp.arange(nb, dtype=jnp.int32)[None, :]
    forced = (j == 0) | (j == cur[:, None]) | (j == cur[:, None] - 1)
    cand = j <= cur[:, None]
    score = jnp.where(forced[:, None, :], FORCED_SCORE, p_cmp.sum(axis=3))
    score = jnp.where(cand[:, None, :], score, -1.0)
    top_val, top_idx = lax.top_k(score, min(SEL_TOP_K, nb))
    kpos = top_idx[..., None] * SEL_BLOCK + jnp.arange(SEL_BLOCK, dtype=jnp.int32)
    rows = fetch_sel(kpos)
    ks = rms_norm(rows[..., 0, :], k_norm[1])
    dist = (qpos[None, :, None, None, None] - kpos).astype(jnp.float32)
    ok = (dist >= 0) & (top_val >= 0)[..., None]
    s = jnp.einsum('btkgd,btknsd->btkgns', qh, ks).astype(jnp.float32) * NSA_SCALE
    s = s - slopes[None, None, :, :, None, None] * dist[:, :, :, None]
    s = jnp.where(ok[:, :, :, None], s, NEG)
    p = jax.nn.softmax(s.reshape(s.shape[:4] + (-1,)), axis=-1).reshape(s.shape)
    o_slc = jnp.einsum('btkgns,btknsd->btkgd', p, rows[..., 1, :].astype(jnp.float32))
    kw = rms_norm(win_rows[:, :, 0], k_norm[2])
    dist = (qpos[:, None] - win_pos[None, :]).astype(jnp.float32)
    ok = (dist >= 0) & (dist <= WINDOW) & (win_pos >= 0)[None, :]
    s = jnp.einsum('btkgd,blkd->btkgl', qh, kw).astype(jnp.float32) * NSA_SCALE
    s = s - slopes[:, :, None] * dist[:, None, None, :]
    s = jnp.where(ok[:, None, None, :], s, NEG)
    o_win = jnp.einsum('btkgl,blkd->btkgd', jax.nn.softmax(s, axis=-1), win_rows[:, :, 1].astype(jnp.float32))
    g = gates.astype(jnp.float32).reshape(B, T, 3, NSA_KV, G, 1)
    o = g[:, :, 0] * o_cmp + g[:, :, 1] * o_slc + g[:, :, 2] * o_win
    return o.reshape(B, T, NSA_HEADS * HEAD_DIM).astype(q.dtype)


def mla_queries(cq, pos, p):
    B, T = cq.shape[:2]
    q = (rms_norm(cq, p['mla_qa_norm']) @ p['mla_w_uq']).reshape(B, T, MLA_HEADS, MLA_NOPE + MLA_ROPE)
    q = jnp.concatenate([q[..., :MLA_NOPE], rope(q[..., MLA_NOPE:], pos)], axis=-1)
    return rms_norm(q, p['mla_q_norm'])


def mla_rows(ckv, kr, pos, p):
    return jnp.concatenate([rms_norm(ckv, p['mla_kva_norm']), rope(kr, pos)], axis=-1)


def mla_keys(rows, p):
    lat, kr = rows[..., :MLA_KV_RANK], rows[..., MLA_KV_RANK:]
    kv = (lat @ p['mla_w_ukv']).reshape(lat.shape[:-1] + (MLA_HEADS, MLA_NOPE + MLA_V))
    kr = jnp.broadcast_to(kr[..., None, :], kv.shape[:-1] + (MLA_ROPE,))
    k = rms_norm(jnp.concatenate([kv[..., :MLA_NOPE], kr], axis=-1), p['mla_k_norm'])
    return k, kv[..., MLA_NOPE:]


def softmax_partial(q, k, v, mask, scale):
    s = jnp.einsum('bthd,blhd->bhtl', q, k).astype(jnp.float32) * scale
    s = jnp.where(mask[:, None], s, NEG)
    m = s.max(axis=-1)
    e = jnp.exp(s - m[..., None]) * mask[:, None]
    return m, e.sum(axis=-1), jnp.einsum('bhtl,blhd->bhtd', e, v.astype(jnp.float32))


def combine_partials(m, l, acc):
    mx = m.max(axis=0)
    w = jnp.exp(m - mx)
    return (acc * w[..., None]).sum(axis=0) / (l * w).sum(axis=0)[..., None]


def sb_attend(q, qpos, k, v, kpos):
    B, T = q.shape[:2]
    G = SB_HEADS // SB_KV
    qg = q.reshape(B, T, SB_KV, G, HEAD_DIM)
    z = jnp.einsum('btkgd,blkd->bkgtl', qg, k).astype(jnp.float32) * HEAD_DIM ** -0.5
    mask = kpos[None, :] < qpos[:, None]
    log_1mb = jnp.where(mask, jax.nn.log_sigmoid(-z), 0.0)
    suffix = lax.cumsum(log_1mb, axis=4, reverse=True) - log_1mb
    a = jnp.where(mask, jnp.exp(jax.nn.log_sigmoid(z) + suffix), 0.0)
    o = jnp.einsum('bkgtl,blkd->btkgd', a, v.astype(jnp.float32))
    return o.reshape(B, T, SB_HEADS * HEAD_DIM).astype(q.dtype)


def merge_branches(mg, o_nsa, o_mla, o_sb, p):
    br = jnp.stack([o_nsa, o_mla, o_sb], axis=2)
    proj = jnp.einsum('btnc,ncd->btnd', br, p['w_branch'])
    return (mg * proj).sum(axis=2) @ p['w_out']


def prompt_mixer(h, p):
    B, S, _ = h.shape
    pos = jnp.arange(S, dtype=jnp.int32)
    nq, nkv, ng, cq, ckv, kr, sq, skv, mg = split_in(h @ p['w_in'])
    cmp_rows, slc_rows, win_rows = nkv[:, :, 0], nkv[:, :, 1], nkv[:, :, 2]
    means = block_means(cmp_rows)
    cmp_k, cmp_v = means[:, :, 0] @ p['phi_k'], means[:, :, 1] @ p['phi_v']
    win_pad = jnp.pad(win_rows, ((0, 0), (WINDOW, 0), (0, 0), (0, 0), (0, 0)))
    fetch = lambda kpos: take_rows(slc_rows, kpos)

    def nsa_blk(i, qb, gb):
        start = i * Q_BLOCK
        qpos = start + jnp.arange(Q_BLOCK, dtype=jnp.int32)
        wrows = lax.dynamic_slice_in_dim(win_pad, start, WINDOW + Q_BLOCK, axis=1)
        wpos = start - WINDOW + jnp.arange(WINDOW + Q_BLOCK, dtype=jnp.int32)
        return nsa_attend(qb, qpos, gb, cmp_k, cmp_v, fetch, wrows, wpos, p['nsa_q_norm'], p['nsa_k_norm'])

    o_nsa = map_query_blocks(nsa_blk, nq, ng)
    q = mla_queries(cq, pos, p)
    rows = mla_rows(ckv, kr, pos, p)
    k, v = mla_keys(rows, p)

    def mla_blk(i, qb):
        qpos = i * Q_BLOCK + jnp.arange(Q_BLOCK, dtype=jnp.int32)
        _, lsum, acc = softmax_partial(qb, k, v, (pos[None, :] <= qpos[:, None])[None], MLA_SCALE)
        return jnp.swapaxes(acc / lsum[..., None], 1, 2)

    o_mla = map_query_blocks(mla_blk, q).reshape(B, S, MLA_HEADS * MLA_V).astype(h.dtype)
    def sb_blk(i, qb):
        qpos = i * Q_BLOCK + jnp.arange(Q_BLOCK, dtype=jnp.int32)
        return sb_attend(qb, qpos, skv[:, :, 0], skv[:, :, 1], pos)

    o_sb = map_query_blocks(sb_blk, sq)
    out = merge_branches(mg, o_nsa, o_mla, o_sb, p)
    win_state = win_rows[:, max(S - WINDOW, 0):]
    return out, (cmp_rows, slc_rows, win_state, rows, skv)


def sample_mixer(h, p, l, cache_nsa_cmp, cache_nsa_slc, win_buf, cache_mla, cache_sb, page_table):
    B, T, _ = h.shape
    n_pages = page_table.shape[1]
    qpos = PAST_LEN + jnp.arange(T, dtype=jnp.int32)
    nq, nkv, ng, cq, ckv, kr, sq, skv, mg = split_in(h @ p['w_in'])
    cmp_rows, slc_rows, win_rows = nkv[:, :, 0], nkv[:, :, 1], nkv[:, :, 2]
    n_new = -(-T // CMP_BLOCK) * CMP_BLOCK
    pad_rows = lambda r: jnp.pad(r, ((0, 0), (0, n_new - T), (0, 0), (0, 0), (0, 0)))
    past_cmp = cache_nsa_cmp[l, page_table].reshape((B, PAST_LEN) + cmp_rows.shape[2:])
    means = jnp.concatenate([block_means(past_cmp), block_means(pad_rows(cmp_rows))], axis=1)
    cmp_k, cmp_v = means[:, :, 0] @ p['phi_k'], means[:, :, 1] @ p['phi_v']
    slc_new = pad_rows(slc_rows)

    def fetch(kpos):
        past = take_paged_rows(cache_nsa_slc, l, page_table, jnp.minimum(kpos, PAST_LEN - 1))
        new = take_rows(slc_new, jnp.clip(kpos - PAST_LEN, 0, n_new - 1))
        return jnp.where((kpos < PAST_LEN)[..., None, None], past, new)

    lw = win_buf.shape[1]
    wrows = jnp.concatenate([win_buf, win_rows], axis=1)
    wpos = jnp.concatenate([PAST_LEN - lw + jnp.arange(lw, dtype=jnp.int32), qpos])
    o_nsa = nsa_attend(nq, qpos, ng, cmp_k, cmp_v, fetch, wrows, wpos, p['nsa_q_norm'], p['nsa_k_norm'])
    q = mla_queries(cq, qpos, p)
    new_rows = mla_rows(ckv, kr, qpos, p)

    def page_part(jp):
        k, v = mla_keys(cache_mla[l, page_table[:, jp]], p)
        return softmax_partial(q, k, v, jnp.ones((1, T, PAGE_SIZE), dtype=bool), MLA_SCALE)

    m_p, l_p, a_p = lax.map(page_part, jnp.arange(n_pages, dtype=jnp.int32))
    k_n, v_n = mla_keys(new_rows, p)
    tt = jnp.arange(T, dtype=jnp.int32)
    m_s, l_s, a_s = softmax_partial(q, k_n, v_n, (tt[None, :] <= tt[:, None])[None], MLA_SCALE)
    o = combine_partials(jnp.concatenate([m_p, m_s[None]], axis=0),
                         jnp.concatenate([l_p, l_s[None]], axis=0),
                         jnp.concatenate([a_p, a_s[None]], axis=0))
    o_mla = jnp.swapaxes(o, 1, 2).reshape(B, T, MLA_HEADS * MLA_V).astype(h.dtype)
    past_sb = cache_sb[l, page_table].reshape((B, PAST_LEN) + skv.shape[2:])
    kv = jnp.concatenate([past_sb, skv], axis=1)
    o_sb = sb_attend(sq, qpos, kv[:, :, 0], kv[:, :, 1], jnp.arange(PAST_LEN + T, dtype=jnp.int32))
    out = merge_branches(mg, o_nsa, o_mla, o_sb, p)
    return out, (cmp_rows, slc_rows, wrows[:, T:], new_rows, skv)


def setup_inputs(seed: int = 0) -> dict:
    key = jax.random.key(seed)
    ks = jax.random.split(key, 32)
    f32 = jnp.float32
    n_pages = PAST_LEN // PAGE_SIZE
    n_used = DEC_BATCH * n_pages
    n_pool = n_used + max(1, n_used // 4)
    win_len = min(WINDOW, PAST_LEN)
    nrm = lambda k, shape, s=1.0: s * jax.random.normal(k, shape, f32)
    gain = lambda k, shape: 1.0 + 0.01 * jax.random.normal(k, shape, f32)
    perm = jax.random.permutation(ks[7], n_pool)
    page_table = perm[:n_used].reshape(DEC_BATCH, n_pages).astype(jnp.int32)
    return {
        'x_prompt': nrm(ks[0], (BATCH, SEQ, D_MODEL)),
        'x_sample': nrm(ks[1], (DEC_BATCH, DEC_SEQ, D_MODEL)),
        'cache_nsa_cmp': nrm(ks[2], (DEPTH, n_pool, PAGE_SIZE, 2, NSA_KV, HEAD_DIM)),
        'cache_nsa_slc': nrm(ks[3], (DEPTH, n_pool, PAGE_SIZE, 2, NSA_KV, HEAD_DIM)),
        'state_nsa_win': nrm(ks[4], (DEPTH, DEC_BATCH, win_len, 2, NSA_KV, HEAD_DIM)),
        'cache_mla': nrm(ks[5], (DEPTH, n_pool, PAGE_SIZE, MLA_KV_RANK + MLA_ROPE)),
        'cache_sb': nrm(ks[6], (DEPTH, n_pool, PAGE_SIZE, 2, SB_KV, HEAD_DIM)),
        'page_table': page_table,
        'c_prompt': nrm(ks[8], (BATCH, D_MODEL)),
        'c_sample': nrm(ks[9], (DEC_BATCH, D_MODEL)),
        'ada_w': nrm(ks[10], (DEPTH, D_MODEL, 6 * D_MODEL), 0.5 * D_MODEL ** -0.5),
        'ada_b': nrm(ks[11], (DEPTH, 6 * D_MODEL), 0.01),
        'norm_mix': gain(ks[12], (DEPTH, D_MODEL)),
        'norm_ffn': gain(ks[13], (DEPTH, D_MODEL)),
        'w_in': nrm(ks[14], (DEPTH, D_MODEL, D_IN), D_MODEL ** -0.5),
        'nsa_q_norm': gain(ks[15], (DEPTH, HEAD_DIM)),
        'nsa_k_norm': gain(ks[16], (DEPTH, 3, HEAD_DIM)),
        'nsa_phi_k': nrm(ks[17], (DEPTH, HEAD_DIM, HEAD_DIM), HEAD_DIM ** -0.5),
        'nsa_phi_v': nrm(ks[18], (DEPTH, HEAD_DIM, HEAD_DIM), HEAD_DIM ** -0.5),
        'mla_qa_norm': gain(ks[19], (DEPTH, MLA_Q_RANK)),
        'mla_kva_norm': gain(ks[20], (DEPTH, MLA_KV_RANK)),
        'mla_w_uq': nrm(ks[21], (DEPTH, MLA_Q_RANK, MLA_HEADS * (MLA_NOPE + MLA_ROPE)), MLA_Q_RANK ** -0.5),
        'mla_w_ukv': nrm(ks[22], (DEPTH, MLA_KV_RANK, MLA_HEADS * (MLA_NOPE + MLA_V)), MLA_KV_RANK ** -0.5),
        'mla_q_norm': gain(ks[23], (DEPTH, MLA_NOPE + MLA_ROPE)),
        'mla_k_norm': gain(ks[24], (DEPTH, MLA_NOPE + MLA_ROPE)),
        'w_branch': nrm(ks[25], (DEPTH, N_BRANCH, BRANCH_WIDTH, D_MODEL), BRANCH_WIDTH ** -0.5),
        'w_out': nrm(ks[26], (DEPTH, D_MODEL, D_MODEL), D_MODEL ** -0.5),
        'ffn_w_up': nrm(ks[27], (DEPTH, D_MODEL, 2 * D_FF), D_MODEL ** -0.5),
        'ffn_w_down': nrm(ks[28], (DEPTH, D_FF, D_MODEL), D_FF ** -0.5),
    }


def reference(x_prompt, x_sample, cache_nsa_cmp, cache_nsa_slc, state_nsa_win, cache_mla, cache_sb,
              page_table, c_prompt, c_sample, ada_w, ada_b, norm_mix, norm_ffn, w_in,
              nsa_q_norm, nsa_k_norm, nsa_phi_k, nsa_phi_v, mla_qa_norm, mla_kva_norm,
              mla_w_uq, mla_w_ukv, mla_q_norm, mla_k_norm, w_branch, w_out, ffn_w_up, ffn_w_down):
    y_p, y_s = x_prompt, x_sample
    cmp_p, cmp_s, slc_p, slc_s, win_p, win_s, mla_p, mla_s, sb_p, sb_s = ([] for _ in range(10))
    for l in range(DEPTH):
        p = {'w_in': w_in[l], 'nsa_q_norm': nsa_q_norm[l], 'nsa_k_norm': nsa_k_norm[l],
             'phi_k': nsa_phi_k[l], 'phi_v': nsa_phi_v[l],
             'mla_qa_norm': mla_qa_norm[l], 'mla_kva_norm': mla_kva_norm[l],
             'mla_w_uq': mla_w_uq[l], 'mla_w_ukv': mla_w_ukv[l],
             'mla_q_norm': mla_q_norm[l], 'mla_k_norm': mla_k_norm[l],
             'w_branch': w_branch[l], 'w_out': w_out[l]}
        sh1, sc1, g1, sh2, sc2, g2 = ada_modulation(c_prompt, ada_w[l], ada_b[l])
        o, (r_cmp, r_slc, r_win, r_mla, r_sb) = prompt_mixer(modulate(y_p, norm_mix[l], sh1, sc1), p)
        y_p = y_p + g1 * o
        y_p = y_p + g2 * swiglu(modulate(y_p, norm_ffn[l], sh2, sc2), ffn_w_up[l], ffn_w_down[l])
        cmp_p.append(r_cmp); slc_p.append(r_slc); win_p.append(r_win); mla_p.append(r_mla); sb_p.append(r_sb)
        sh1, sc1, g1, sh2, sc2, g2 = ada_modulation(c_sample, ada_w[l], ada_b[l])
        o, (r_cmp, r_slc, r_win, r_mla, r_sb) = sample_mixer(
            modulate(y_s, norm_mix[l], sh1, sc1), p, l, cache_nsa_cmp, cache_nsa_slc,
            state_nsa_win[l], cache_mla, cache_sb, page_table)
        y_s = y_s + g1 * o
        y_s = y_s + g2 * swiglu(modulate(y_s, norm_ffn[l], sh2, sc2), ffn_w_up[l], ffn_w_down[l])
        cmp_s.append(r_cmp); slc_s.append(r_slc); win_s.append(r_win); mla_s.append(r_mla); sb_s.append(r_sb)
    nsa_cmp_p, nsa_cmp_s = jnp.stack(cmp_p), jnp.stack(cmp_s)
    nsa_slc_p, nsa_slc_s = jnp.stack(slc_p), jnp.stack(slc_s)
    nsa_win_p, nsa_win_s = jnp.stack(win_p), jnp.stack(win_s)
    mla_new_p, mla_new_s = jnp.stack(mla_p), jnp.stack(mla_s)
    sb_new_p, sb_new_s = jnp.stack(sb_p), jnp.stack(sb_s)
    return (y_p, y_s, nsa_cmp_p, nsa_cmp_s, nsa_slc_p, nsa_slc_s, nsa_win_p, nsa_win_s,
            mla_new_p, mla_new_s, sb_new_p, sb_new_s)
```

```python
import functools

import jax
import jax.numpy as jnp
import numpy as np
from jax import lax
from jax.experimental import pallas as pl
from jax.experimental.pallas import tpu as pltpu

D_MODEL = 1024
DEPTH = 2
PAST_LEN = 16384
PAGE_SIZE = 128
HEAD_DIM = 64
NSA_HEADS = 8
CMP_BLOCK = 64
SEL_BLOCK = 64
SEL_TOP_K = 16
WINDOW = 512
MLA_HEADS = 8
MLA_Q_RANK = 256
MLA_KV_RANK = 128
MLA_NOPE = 64
MLA_ROPE = 32
MLA_V = 64
ROPE_THETA = 10000.0
SB_HEADS = 8
SB_KV = 2
N_BRANCH = 3
BRANCH_WIDTH = 512
D_FF = ((8 * D_MODEL + 3 * 256 - 1) // (3 * 256)) * 256
EPS = 1e-6
NEG = -1e30
FORCED_SCORE = 1e4
NSA_SCALE = HEAD_DIM ** -0.5
MLA_SCALE = (MLA_NOPE + MLA_ROPE) ** -0.5
SB_SCALE = HEAD_DIM ** -0.5
IN_SIZES = (512, 384, 24, 256, 128, 32, 512, 256, 3072)
IN_OFFS = tuple(int(v) for v in np.cumsum((0,) + IN_SIZES))

LANES = 128
VMEM_LIMIT = 56 * 1024 * 1024
BF16 = jnp.bfloat16
F32 = jnp.float32


def _cparams(n_axes):
    return pltpu.CompilerParams(dimension_semantics=("parallel",) * n_axes,
                                vmem_limit_bytes=VMEM_LIMIT)


def _resident(shape):
    zeros = (0,) * len(shape)
    return pl.BlockSpec(shape, lambda *_: zeros, pipeline_mode=pl.Buffered(1))


def _lane_lt64(shape):
    return lax.broadcasted_iota(jnp.int32, shape, len(shape) - 1) % LANES < 64


def _dot(a, b):
    return jnp.dot(a, b, preferred_element_type=F32)


def _dot_t(a, b):
    return lax.dot_general(a, b, (((1,), (1,)), ((), ())), preferred_element_type=F32)


def _modulated(x, g, shift, scale):
    y = x * lax.rsqrt(jnp.mean(x * x, axis=-1, keepdims=True) + EPS) * g
    return y * (1.0 + scale) + shift


def _rope128(x, c, s1, s2):
    return x * c + pltpu.roll(x, LANES - 16, 1) * s1 + pltpu.roll(x, 16, 1) * s2


def _dup_halves(rows):
    r = pltpu.roll(rows, 64, 1)
    lt = _lane_lt64(rows.shape)
    return jnp.where(lt, rows, r), jnp.where(lt, r, rows)


def _in_proj_kernel(x_ref, mod_ref, g_ref, wp_ref, seg_ref, nqg_ref, nkg_ref, qag_ref, kvag_ref,
                    wuq_ref, wuk_ref, wuv_ref, mqg_ref, mkg_ref, rope_ref,
                    cmp_ref, slc_ref, win_ref, sbrow_ref, mlarow_ref, gate_ref,
                    nq_ref, slck_ref, slcv_ref, wink_ref, winv_ref,
                    mq_ref, mk_ref, mv_ref, sq_ref, sbk_ref, sbv_ref):
    x = x_ref[0]
    h = _modulated(x, g_ref[...], mod_ref[0, 0], mod_ref[0, 1]).astype(BF16)
    z = _dot(h, wp_ref[...])
    nq, nkv = z[:, 0:512], z[:, 512:896]
    cq, ckv, krp = z[:, 896:1152], z[:, 1152:1280], z[:, 1280:1408]
    sq, skv, ng = z[:, 1408:1920], z[:, 1920:2176], z[:, 2176:2304]
    cmp_ref[0] = nkv[:, 0:128]
    slc_ref[0] = nkv[:, 128:256]
    win_ref[0] = nkv[:, 256:384]
    sbrow_ref[0] = skv
    gate_ref[0] = jax.nn.sigmoid(ng)
    ss = _dot((nq * nq).astype(BF16), seg_ref[...])
    nq_ref[0] = (nq * lax.rsqrt(ss * (1.0 / HEAD_DIM) + EPS) * nqg_ref[...]).astype(BF16)
    for rows, kg, k_out, v_out in ((nkv[:, 128:256], nkg_ref[1:2], slck_ref, slcv_ref),
                                   (nkv[:, 256:384], nkg_ref[2:3], wink_ref, winv_ref)):
        k2, v2 = _dup_halves(rows)
        kss = jnp.sum(k2 * k2, axis=-1, keepdims=True) * (1.0 / LANES)
        k_out[0] = (k2 * lax.rsqrt(kss + EPS) * kg).astype(BF16)
        v_out[0] = v2.astype(BF16)
    sq_ref[0] = (sq * SB_SCALE).astype(BF16)
    k01, v01 = skv[:, 0:128], skv[:, 128:256]
    ka, kb = _dup_halves(k01)
    va, vb = _dup_halves(v01)
    sbk_ref[0] = jnp.concatenate([ka, kb], axis=1).astype(BF16)
    sbv_ref[0] = jnp.concatenate([va, vb], axis=1).astype(BF16)
    rc, rs1, rs2 = rope_ref[0], rope_ref[1], rope_ref[2]
    lat = ckv * lax.rsqrt(jnp.mean(ckv * ckv, axis=-1, keepdims=True) + EPS) * kvag_ref[...]
    kr = _rope128(krp, rc, rs1, rs2)
    mlarow_ref[0, :, 0:128] = lat
    mlarow_ref[0, :, 128:160] = kr[:, 64:96]
    latb = lat.astype(BF16)
    kn = _dot(latb, wuk_ref[...])
    mv_ref[0] = _dot(latb, wuv_ref[...]).astype(BF16)
    cqn = (cq * lax.rsqrt(jnp.mean(cq * cq, axis=-1, keepdims=True) + EPS) * qag_ref[...]).astype(BF16)
    qf = _dot(cqn, wuq_ref[...])
    inv_d = 1.0 / (MLA_NOPE + MLA_ROPE)
    for hd in range(MLA_HEADS):
        sl = slice(hd * LANES, (hd + 1) * LANES)
        kh = kn[:, sl] + kr
        kss = jnp.sum(kh * kh, axis=-1, keepdims=True) * inv_d
        mk_ref[0, :, sl] = (kh * lax.rsqrt(kss + EPS) * mkg_ref[...]).astype(BF16)
        qh = _rope128(qf[:, sl], rc, rs1, rs2)
        qss = jnp.sum(qh * qh, axis=-1, keepdims=True) * inv_d
        mq_ref[0, :, sl] = (qh * lax.rsqrt(qss + EPS) * mqg_ref[...]).astype(BF16)


def _pad_heads(w, n_heads, parts, width=LANES):
    k = w.shape[0]
    w = w.reshape(k, n_heads, sum(parts))
    return jnp.pad(w, ((0, 0), (0, 0), (0, width - sum(parts)))).reshape(k, n_heads * width)


def _rope_tables(pos):
    half = MLA_ROPE // 2
    freq = ROPE_THETA ** (-jnp.arange(half, dtype=F32) / half)
    ang = pos.astype(F32)[:, None] * freq
    cos, sin = jnp.cos(ang), jnp.sin(ang)
    t = pos.shape[0]
    one, zero = jnp.ones((t, 64), F32), jnp.zeros((t, 64), F32)
    z16, z32 = jnp.zeros((t, 16), F32), jnp.zeros((t, 32), F32)
    c = jnp.concatenate([one, cos, cos, z32], axis=1)
    s1 = jnp.concatenate([zero, -sin, z16, z32], axis=1)
    s2 = jnp.concatenate([zero, z16, sin, z32], axis=1)
    return jnp.stack([c, s1, s2])


def _layer_weights(l, norm_mix, norm_ffn, w_in, nsa_q_norm, nsa_k_norm, mla_qa_norm, mla_kva_norm,
                   mla_w_uq, mla_w_ukv, mla_q_norm, mla_k_norm, w_branch, w_out, ffn_w_up, ffn_w_down):
    o = IN_OFFS
    w = w_in[l]
    col = lambda i: w[:, o[i]:o[i + 1]]
    zpad = lambda n: jnp.zeros((D_MODEL, n), F32)
    wp = jnp.concatenate([col(0), col(1), col(3), col(4), zpad(64), col(5), zpad(32),
                          col(6), col(7), col(2), zpad(LANES - 24)], axis=1).astype(BF16)
    seg = jnp.asarray(np.kron(np.eye(NSA_HEADS), np.ones((HEAD_DIM, HEAD_DIM))), BF16)
    d_qk = MLA_NOPE + MLA_ROPE
    wuq = _pad_heads(mla_w_uq[l], MLA_HEADS, (d_qk,)).astype(BF16)
    wukv = mla_w_ukv[l].reshape(MLA_KV_RANK, MLA_HEADS, MLA_NOPE + MLA_V)
    wuk = _pad_heads(wukv[:, :, :MLA_NOPE].reshape(MLA_KV_RANK, -1), MLA_HEADS, (MLA_NOPE,)).astype(BF16)
    wuv = wukv[:, :, MLA_NOPE:].reshape(MLA_KV_RANK, -1).astype(BF16)
    pad_g = lambda g: jnp.pad(g, (0, LANES - d_qk)).reshape(1, LANES)
    return dict(
        g_mix=norm_mix[l].reshape(1, D_MODEL), g_ffn=norm_ffn[l].reshape(1, D_MODEL),
        wp=wp, seg=seg,
        nqg=(jnp.tile(nsa_q_norm[l], NSA_HEADS) * NSA_SCALE).reshape(1, 512),
        nkg=jnp.tile(nsa_k_norm[l], (1, 2)),
        qag=mla_qa_norm[l].reshape(1, MLA_Q_RANK), kvag=mla_kva_norm[l].reshape(1, MLA_KV_RANK),
        wuq=wuq, wuk=wuk, wuv=wuv,
        mqg=pad_g(mla_q_norm[l]) * MLA_SCALE, mkg=pad_g(mla_k_norm[l]),
        wmg=w[:, o[8]:o[9]].astype(BF16),
        wbr=w_branch[l].astype(BF16), wout=w_out[l].astype(BF16),
        wup=ffn_w_up[l].astype(BF16), wdn=ffn_w_down[l].astype(BF16))


def in_proj(x, shift, scale, lw, rope_tab, bm):
    B, T, D = x.shape
    R = shift.shape[1]
    mod = jnp.stack([shift, scale], axis=1)
    rb = 1 if R == 1 else bm
    mod_map = (lambda b, i: (b, 0, 0, 0)) if R == 1 else (lambda b, i: (b, 0, i, 0))
    row = lambda w: pl.BlockSpec((1, bm, w), lambda b, i: (b, i, 0))
    outs = [("cmp", 128, F32), ("slc", 128, F32), ("win", 128, F32), ("sb_rows", 256, F32),
            ("mla_rows", MLA_KV_RANK + MLA_ROPE, F32), ("gates", 128, F32),
            ("nq", 512, BF16), ("slc_k", 128, BF16), ("slc_v", 128, BF16),
            ("win_k", 128, BF16), ("win_v", 128, BF16),
            ("mla_q", 1024, BF16), ("mla_k", 1024, BF16), ("mla_v", 512, BF16),
            ("sb_q", 512, BF16), ("sb_k", 256, BF16), ("sb_v", 256, BF16)]
    res = pl.pallas_call(
        _in_proj_kernel,
        grid=(B, T // bm),
        in_specs=[row(D), pl.BlockSpec((1, 2, rb, D), mod_map), _resident((1, D)),
                  _resident(lw["wp"].shape), _resident(lw["seg"].shape), _resident((1, 512)),
                  _resident((3, 128)), _resident((1, MLA_Q_RANK)), _resident((1, MLA_KV_RANK)),
                  _resident(lw["wuq"].shape), _resident(lw["wuk"].shape), _resident(lw["wuv"].shape),
                  _resident((1, LANES)), _resident((1, LANES)),
                  pl.BlockSpec((3, bm, LANES), lambda b, i: (0, i, 0))],
        out_specs=[row(w) for _, w, _ in outs],
        out_shape=[jax.ShapeDtypeStruct((B, T, w), dt) for _, w, dt in outs],
        compiler_params=_cparams(2),
        name="in_proj",
    )(x, mod, lw["g_mix"], lw["wp"], lw["seg"], lw["nqg"], lw["nkg"], lw["qag"], lw["kvag"],
      lw["wuq"], lw["wuk"], lw["wuv"], lw["mqg"], lw["mkg"], rope_tab)
    return {name: r for (name, _, _), r in zip(outs, res)}


def _softmax_step(s, v, m, l, acc):
    m_new = jnp.maximum(m, jnp.max(s, axis=-1, keepdims=True))
    alpha = jnp.exp(m - m_new)
    p = jnp.exp(s - m_new)
    l_new = alpha * l + jnp.sum(p, axis=-1, keepdims=True)
    pv = _dot(p.reshape(-1, p.shape[-1]).astype(BF16), v).reshape(acc.shape)
    return m_new, l_new, alpha * acc + pv


def _mla_attn_kernel(q_ref, k_ref, v_ref, o_ref, *, bq):
    i = pl.program_id(2)
    row = lax.broadcasted_iota(jnp.int32, (bq, bq), 0)
    col = lax.broadcasted_iota(jnp.int32, (bq, bq), 1)
    outs = []
    for hh in range(2):
        lanes = slice(hh * LANES, (hh + 1) * LANES)
        q = q_ref[0, :, lanes]

        def tile(t, carry, diagonal):
            k = k_ref[0, pl.ds(t * bq, bq), lanes]
            v = v_ref[0, pl.ds(t * bq, bq), :]
            s = _dot_t(q, k)
            if diagonal:
                s = jnp.where(col <= row, s, NEG)
            return _softmax_step(s, v, *carry)

        init = (jnp.full((bq, 1), NEG, F32), jnp.zeros((bq, 1), F32), jnp.zeros((bq, LANES), F32))
        carry = lax.fori_loop(0, i, lambda t, c: tile(t, c, False), init)
        m, l, acc = tile(i, carry, True)
        outs.append(acc / l)
    o_ref[0] = jnp.where(_lane_lt64((bq, LANES)), outs[0], outs[1])


def mla_attention(q, k, v, bq):
    B, S, _ = q.shape
    return pl.pallas_call(
        functools.partial(_mla_attn_kernel, bq=bq),
        grid=(B, MLA_HEADS // 2, S // bq),
        in_specs=[pl.BlockSpec((1, bq, 2 * LANES), lambda b, j, i: (b, i, j)),
                  pl.BlockSpec((1, S, 2 * LANES), lambda b, j, i: (b, 0, j)),
                  pl.BlockSpec((1, S, LANES), lambda b, j, i: (b, 0, j))],
        out_specs=pl.BlockSpec((1, bq, LANES), lambda b, j, i: (b, i, j)),
        out_shape=jax.ShapeDtypeStruct((B, S, MLA_HEADS * MLA_V), F32),
        compiler_params=_cparams(3),
        name="mla_attention",
    )(q, k, v)


def _softplus(z):
    return jnp.maximum(z, 0.0) + jnp.log1p(jnp.exp(-jnp.abs(z)))


def _sb_attn_kernel(q_ref, k_ref, v_ref, tri_ref, o_ref, *, bq):
    i = pl.program_id(2)
    n = SB_HEADS // SB_KV
    row = lax.broadcasted_iota(jnp.int32, (bq, bq), 0)
    col = lax.broadcasted_iota(jnp.int32, (bq, bq), 1)
    lt = _lane_lt64((bq, LANES))
    qs = []
    for h in range(n):
        qp = q_ref[0, :, (h // 2) * LANES:(h // 2 + 1) * LANES]
        qs.append(jnp.where(lt == (h % 2 == 0), qp, jnp.zeros_like(qp)))
    q = jnp.concatenate(qs, axis=0)
    tri = tri_ref[...]

    def tile(t, carry, diagonal):
        c, acc = carry
        k = k_ref[0, pl.ds(t * bq, bq), :]
        v = v_ref[0, pl.ds(t * bq, bq), :]
        z = _dot_t(q, k).reshape(n, bq, bq)
        sp = _softplus(z)
        if diagonal:
            valid = (col < row)[None]
            sp = jnp.where(valid, sp, 0.0)
        sp2 = sp.reshape(n * bq, bq)
        hi = sp2.astype(BF16)
        lo = (sp2 - hi.astype(F32)).astype(BF16)
        cum = (_dot(hi, tri) + _dot(lo, tri)).reshape(n, bq, bq)
        a = jnp.exp(z - cum - c)
        if diagonal:
            a = jnp.where(valid, a, 0.0)
        acc = acc + _dot(a.reshape(n * bq, bq).astype(BF16), v).reshape(n, bq, LANES)
        return c + cum[:, :, 0:1], acc

    carry = tile(i, (jnp.zeros((n, bq, 1), F32), jnp.zeros((n, bq, LANES), F32)), True)
    _, acc = lax.fori_loop(0, i, lambda u, cr: tile(i - 1 - u, cr, False), carry)
    for j in range(n // 2):
        o_ref[0, :, j * LANES:(j + 1) * LANES] = jnp.where(lt, acc[2 * j], acc[2 * j + 1])


def sb_attention(q, k2, v2, bq):
    B, S, _ = q.shape
    tri = jnp.asarray(np.tril(np.ones((bq, bq), np.float32)), BF16)
    w = (SB_HEADS // SB_KV) * HEAD_DIM
    return pl.pallas_call(
        functools.partial(_sb_attn_kernel, bq=bq),
        grid=(B, SB_KV, S // bq),
        in_specs=[pl.BlockSpec((1, bq, w), lambda b, g, i: (b, i, g)),
                  pl.BlockSpec((1, S, LANES), lambda b, g, i: (b, 0, g)),
                  pl.BlockSpec((1, S, LANES), lambda b, g, i: (b, 0, g)),
                  _resident((bq, bq))],
        out_specs=pl.BlockSpec((1, bq, w), lambda b, g, i: (b, i, g)),
        out_shape=jax.ShapeDtypeStruct((B, S, SB_HEADS * HEAD_DIM), F32),
        compiler_params=_cparams(3),
        name="sb_attention",
    )(q, k2, v2, tri)


def _cmp_blocks_kernel(rows_ref, wk_ref, wv_ref, g_ref, kc_ref, vc_ref):
    s = rows_ref.shape[1]
    nb = s // CMP_BLOCK
    means = rows_ref[0].reshape(nb, CMP_BLOCK, LANES).sum(axis=1) * (1.0 / CMP_BLOCK)
    mb = means.astype(BF16)
    k2 = _dot(mb, wk_ref[...])
    kss = jnp.sum(k2 * k2, axis=-1, keepdims=True) * (1.0 / LANES)
    kc_ref[0] = (k2 * lax.rsqrt(kss + EPS) * g_ref[...]).astype(BF16)
    vc_ref[0] = _dot(mb, wv_ref[...]).astype(BF16)


def cmp_blocks(rows, phi_k, phi_v, k_gain):
    B, S, _ = rows.shape
    nb = S // CMP_BLOCK
    z = jnp.zeros((HEAD_DIM, LANES), F32)
    wk = jnp.concatenate([jnp.tile(phi_k, (1, 2)), z], axis=0).astype(BF16)
    wv = jnp.concatenate([z, jnp.tile(phi_v, (1, 2))], axis=0).astype(BF16)
    return pl.pallas_call(
        _cmp_blocks_kernel,
        grid=(B,),
        in_specs=[pl.BlockSpec((1, S, LANES), lambda b: (b, 0, 0)),
                  _resident((LANES, LANES)), _resident((LANES, LANES)), _resident((1, LANES))],
        out_specs=[pl.BlockSpec((1, nb, LANES), lambda b: (b, 0, 0))] * 2,
        out_shape=[jax.ShapeDtypeStruct((B, nb, LANES), BF16)] * 2,
        compiler_params=_cparams(1),
        name="cmp_blocks",
    )(rows, wk, wv, jnp.tile(k_gain, 2).reshape(1, LANES))


def _nsa_attn_kernel(q_ref, kc_ref, vc_ref, ks_ref, vs_ref, kw_ref, vw_ref, blk_ref, gate_ref,
                     gexp_ref, o_ref, ms_ref, ls_ref, as_ref, mw_ref, lw_ref, aw_ref, *, bq):
    i = pl.program_id(1)
    nh = NSA_HEADS
    nb = kc_ref.shape[1]
    lt = _lane_lt64((bq, LANES))
    qs = []
    for h in range(nh):
        qp = q_ref[0, :, (h // 2) * LANES:(h // 2 + 1) * LANES]
        qs.append(jnp.where(lt == (h % 2 == 0), qp, jnp.zeros_like(qp)))
    q = jnp.concatenate(qs, axis=0)
    hidx = lax.broadcasted_iota(jnp.int32, (nh, 1, 1), 0).astype(F32)
    slope = jnp.exp2(-(hidx + 1.0))

    qpos_c = i * bq + lax.broadcasted_iota(jnp.int32, (bq, nb), 0)
    blk = lax.broadcasted_iota(jnp.int32, (bq, nb), 1)
    dist_c = (qpos_c - ((blk + 1) * CMP_BLOCK - 1)).astype(F32)
    ok_c = dist_c >= 0
    s = _dot_t(q, kc_ref[0]).reshape(nh, bq, nb) - slope * dist_c[None]
    s = jnp.where(ok_c[None], s, NEG)
    e = jnp.exp(s - jnp.max(s, axis=-1, keepdims=True))
    p_cmp = e / jnp.sum(e, axis=-1, keepdims=True) * ok_c[None].astype(F32)
    o_cmp = _dot(p_cmp.reshape(nh * bq, nb).astype(BF16), vc_ref[0]).reshape(nh, bq, LANES)
    cur = qpos_c // SEL_BLOCK
    forced = (blk == 0) | (blk == cur) | (blk == cur - 1)
    score = jnp.where(forced, FORCED_SCORE, jnp.sum(p_cmp, axis=0))
    score = jnp.where(blk <= cur, score, -1.0)
    blkf = blk.astype(F32)
    sel = jnp.zeros((bq, nb), F32)
    for _ in range(SEL_TOP_K):
        mx = jnp.max(score, axis=-1, keepdims=True)
        idx = jnp.min(jnp.where(score == mx, blkf, float(nb)), axis=-1, keepdims=True)
        hit = blkf == idx
        sel = jnp.where(hit & (mx >= 0.0), 1.0, sel)
        score = jnp.where(hit, -2.0, score)
    selb = sel.astype(BF16)

    row = i * bq + lax.broadcasted_iota(jnp.int32, (bq, bq), 0)
    col = lax.broadcasted_iota(jnp.int32, (bq, bq), 1)

    def tile(t, k_ref, v_ref, m_ref, l_ref, a_ref, window):
        k = k_ref[0, pl.ds(t * bq, bq), :]
        v = v_ref[0, pl.ds(t * bq, bq), :]
        dist = (row - (t * bq + col)).astype(F32)
        if window:
            ok = (dist >= 0) & (dist <= WINDOW)
        else:
            picked = _dot(selb, blk_ref[:, pl.ds(t * bq, bq)])
            ok = (dist >= 0) & (picked > 0.5)
        s = _dot_t(q, k).reshape(nh, bq, bq) - slope * dist[None]
        s = jnp.where(ok[None], s, NEG)
        m, l, a = _softmax_step(s, v, m_ref[...], l_ref[...], a_ref[...])
        m_ref[...] = m
        l_ref[...] = l
        a_ref[...] = a

    for m_ref, l_ref, a_ref in ((ms_ref, ls_ref, as_ref), (mw_ref, lw_ref, aw_ref)):
        m_ref[...] = jnp.full(m_ref.shape, NEG, F32)
        l_ref[...] = jnp.zeros(l_ref.shape, F32)
        a_ref[...] = jnp.zeros(a_ref.shape, F32)

    def slc_tile(t, carry):
        tile(t, ks_ref, vs_ref, ms_ref, ls_ref, as_ref, False)
        return carry

    def win_tile(t, carry):
        tile(t, kw_ref, vw_ref, mw_ref, lw_ref, aw_ref, True)
        return carry

    lax.fori_loop(0, i + 1, slc_tile, 0)
    lax.fori_loop(jnp.maximum(i - pl.cdiv(WINDOW, bq), 0), i + 1, win_tile, 0)

    g = gate_ref[0]
    ghi = g.astype(BF16)
    glo = (g - ghi.astype(F32)).astype(BF16)
    gx = _dot(ghi, gexp_ref[...]) + _dot(glo, gexp_ref[...])
    o_slc = as_ref[...] / ls_ref[...]
    o_win = aw_ref[...] / lw_ref[...]
    w = NSA_HEADS * HEAD_DIM
    for j in range(nh // 2):
        lanes = slice(j * LANES, (j + 1) * LANES)
        pair = lambda o: jnp.where(lt, o[2 * j], o[2 * j + 1])
        o_ref[0, :, lanes] = (gx[:, lanes] * pair(o_cmp)
                              + gx[:, w + j * LANES:w + (j + 1) * LANES] * pair(o_slc)
                              + gx[:, 2 * w + j * LANES:2 * w + (j + 1) * LANES] * pair(o_win))


def nsa_attention(q, kc2, vc2, slc_k, slc_v, win_k, win_v, gates, bq):
    B, S, _ = q.shape
    nb = S // SEL_BLOCK
    blk_of_key = np.arange(S)[None, :] // SEL_BLOCK == np.arange(nb)[:, None]
    blk_exp = jnp.asarray(blk_of_key.astype(np.float32), BF16)
    gexp = np.zeros((LANES, 3 * 512), np.float32)
    for t in range(3):
        for h in range(NSA_HEADS):
            gexp[t * 8 + h, t * 512 + h * 64:t * 512 + (h + 1) * 64] = 1.0
    tile_spec = pl.BlockSpec((1, bq, LANES), lambda b, i: (b, i, 0))
    seq = lambda n: pl.BlockSpec((1, n, LANES), lambda b, i: (b, 0, 0))
    nh = NSA_HEADS
    return pl.pallas_call(
        functools.partial(_nsa_attn_kernel, bq=bq),
        grid=(B, S // bq),
        in_specs=[pl.BlockSpec((1, bq, 512), lambda b, i: (b, i, 0)), seq(nb), seq(nb),
                  seq(S), seq(S), seq(S), seq(S), _resident((nb, S)), tile_spec,
                  _resident((LANES, 3 * 512))],
        out_specs=pl.BlockSpec((1, bq, 512), lambda b, i: (b, i, 0)),
        out_shape=jax.ShapeDtypeStruct((B, S, 512), F32),
        scratch_shapes=[pltpu.VMEM((nh, bq, 1), F32), pltpu.VMEM((nh, bq, 1), F32),
                        pltpu.VMEM((nh, bq, LANES), F32)] * 2,
        compiler_params=_cparams(2),
        name="nsa_attention",
    )(q, kc2, vc2, slc_k, slc_v, win_k, win_v, blk_exp, gates, jnp.asarray(gexp, BF16))


def _merge_kernel(x_ref, mod_ref, g_ref, wmg_ref, on_ref, om_ref, os_ref, wbr_ref, wout_ref, o_ref):
    x = x_ref[0]
    h = _modulated(x, g_ref[...], mod_ref[0, 0], mod_ref[0, 1]).astype(BF16)
    d = x.shape[-1]
    merged = jnp.zeros(x.shape, F32)
    for n, br_ref in enumerate((on_ref, om_ref, os_ref)):
        gate = jax.nn.sigmoid(_dot(h, wmg_ref[:, n * d:(n + 1) * d]))
        merged = merged + gate * _dot(br_ref[0].astype(BF16), wbr_ref[n])
    o_ref[0] = x + mod_ref[0, 2] * _dot(merged.astype(BF16), wout_ref[...])


def _mod_specs(mods, bm):
    R = mods.shape[2]
    rb = 1 if R == 1 else bm
    mod_map = (lambda b, i: (b, 0, 0, 0)) if R == 1 else (lambda b, i: (b, 0, i, 0))
    return pl.BlockSpec((1, mods.shape[1], rb, mods.shape[3]), mod_map)


def merge_block(x, shift, scale, gate, lw, o_nsa, o_mla, o_sb, bm):
    B, T, D = x.shape
    mod = jnp.stack([shift, scale, gate], axis=1)
    row = lambda w: pl.BlockSpec((1, bm, w), lambda b, i: (b, i, 0))
    return pl.pallas_call(
        _merge_kernel,
        grid=(B, T // bm),
        in_specs=[row(D), _mod_specs(mod, bm), _resident((1, D)), _resident(lw["wmg"].shape),
                  row(BRANCH_WIDTH), row(BRANCH_WIDTH), row(BRANCH_WIDTH),
                  _resident(lw["wbr"].shape), _resident(lw["wout"].shape)],
        out_specs=row(D),
        out_shape=jax.ShapeDtypeStruct(x.shape, F32),
        compiler_params=_cparams(2),
        name="merge_block",
    )(x, mod, lw["g_mix"], lw["wmg"], o_nsa, o_mla, o_sb, lw["wbr"], lw["wout"])


FFN_CHUNKS = 4


def _ffn_kernel(x_ref, mod_ref, g_ref, wup_ref, wdn_ref, o_ref):
    x = x_ref[0]
    h = _modulated(x, g_ref[...], mod_ref[0, 0], mod_ref[0, 1]).astype(BF16)
    dff = wdn_ref.shape[0]
    ck = dff // FFN_CHUNKS
    acc = jnp.zeros(x.shape, F32)
    for c in range(FFN_CHUNKS):
        a = _dot(h, wup_ref[:, c * ck:(c + 1) * ck])
        b = _dot(h, wup_ref[:, dff + c * ck:dff + (c + 1) * ck])
        acc = acc + _dot((a * jax.nn.sigmoid(a) * b).astype(BF16), wdn_ref[c * ck:(c + 1) * ck, :])
    o_ref[0] = x + mod_ref[0, 2] * acc


def ffn_block(x, shift, scale, gate, lw, bm):
    B, T, D = x.shape
    mod = jnp.stack([shift, scale, gate], axis=1)
    row = pl.BlockSpec((1, bm, D), lambda b, i: (b, i, 0))
    return pl.pallas_call(
        _ffn_kernel,
        grid=(B, T // bm),
        in_specs=[row, _mod_specs(mod, bm), _resident((1, D)),
                  _resident(lw["wup"].shape), _resident(lw["wdn"].shape)],
        out_specs=row,
        out_shape=jax.ShapeDtypeStruct(x.shape, F32),
        compiler_params=_cparams(2),
        name="ffn_block",
    )(x, mod, lw["g_ffn"], lw["wup"], lw["wdn"])


def _ada(c, w, b):
    mod = jax.nn.silu(c) @ w + b
    return jnp.split(mod[:, None, :], 6, axis=-1)


def prompt_layer(y, c, ada_w, ada_b, lw, phi_k, phi_v, nsa_k_norm, rope_tab, bm=256, bq=256):
    B, S, _ = y.shape
    sh1, sc1, g1, sh2, sc2, g2 = _ada(c, ada_w, ada_b)
    r = in_proj(y, sh1, sc1, lw, rope_tab, bm)
    kc2, vc2 = cmp_blocks(r["cmp"], phi_k, phi_v, nsa_k_norm[0])
    o_nsa = nsa_attention(r["nq"], kc2, vc2, r["slc_k"], r["slc_v"], r["win_k"], r["win_v"],
                          r["gates"], bq)
    o_mla = mla_attention(r["mla_q"], r["mla_k"], r["mla_v"], bq)
    o_sb = sb_attention(r["sb_q"], r["sb_k"], r["sb_v"], bq)
    y = merge_block(y, sh1, sc1, g1, lw, o_nsa, o_mla, o_sb, bm)
    y = ffn_block(y, sh2, sc2, g2, lw, bm)
    rows5 = lambda a: a.reshape(B, S, 2, 1, HEAD_DIM)
    caches = (rows5(r["cmp"]), rows5(r["slc"]), rows5(r["win"])[:, max(S - WINDOW, 0):],
              r["mla_rows"], r["sb_rows"].reshape(B, S, 2, SB_KV, HEAD_DIM))
    return y, caches


def _rms(x, g):
    return x * lax.rsqrt(jnp.mean(x * x, axis=-1, keepdims=True) + EPS) * g


def _decode_nsa(r, l, cache_cmp, cache_slc, win_state, page_table, phi_k, phi_v, k_norm):
    nb_, n_pages = page_table.shape
    q = r["nq"][0].astype(F32).reshape(nb_, NSA_HEADS, HEAD_DIM)
    slope = 2.0 ** -(jnp.arange(NSA_HEADS, dtype=F32) + 1.0)
    past = cache_cmp[l][page_table].reshape(nb_, PAST_LEN // CMP_BLOCK, CMP_BLOCK, 2 * HEAD_DIM)
    means = jnp.concatenate([past.mean(axis=2), r["cmp"][0][:, None, :] / CMP_BLOCK], axis=1)
    kc = _rms(means[..., :HEAD_DIM] @ phi_k, k_norm[0])
    cv = means[..., HEAD_DIM:] @ phi_v
    nb = means.shape[1]
    j = jnp.arange(nb, dtype=jnp.int32)
    dist = (PAST_LEN - ((j + 1) * CMP_BLOCK - 1)).astype(F32)
    ok = dist >= 0
    s = jnp.einsum('bhd,bnd->bhn', q, kc) - slope[:, None] * dist
    s = jnp.where(ok, s, NEG)
    p_cmp = jax.nn.softmax(s, axis=-1) * ok
    o_cmp = jnp.einsum('bhn,bnd->bhd', p_cmp, cv)
    cur = PAST_LEN // SEL_BLOCK
    forced = (j == 0) | (j == cur) | (j == cur - 1)
    score = jnp.where(forced, FORCED_SCORE, p_cmp.sum(axis=1))
    score = jnp.where(j <= cur, score, -1.0)
    top_val, top_idx = lax.top_k(score, SEL_TOP_K)
    kpos = top_idx[..., None] * SEL_BLOCK + jnp.arange(SEL_BLOCK, dtype=jnp.int32)
    kp = jnp.minimum(kpos, PAST_LEN - 1)
    b_ix = jnp.arange(nb_)[:, None, None]
    rows = cache_slc[l].reshape(-1, PAGE_SIZE, 2 * HEAD_DIM)[page_table[b_ix, kp // PAGE_SIZE], kp % PAGE_SIZE]
    new = jnp.where((kpos == PAST_LEN)[..., None], r["slc"][0][:, None, None, :], 0.0)
    rows = jnp.where((kpos < PAST_LEN)[..., None], rows, new)
    ks = _rms(rows[..., :HEAD_DIM], k_norm[1])
    dist = (PAST_LEN - kpos).astype(F32)
    ok = (dist >= 0) & (top_val >= 0)[..., None]
    s = jnp.einsum('bhd,bnsd->bhns', q, ks) - slope[:, None, None] * dist[:, None]
    s = jnp.where(ok[:, None], s, NEG)
    p = jax.nn.softmax(s.reshape(nb_, NSA_HEADS, -1), axis=-1).reshape(s.shape)
    o_slc = jnp.einsum('bhns,bnsd->bhd', p, rows[..., HEAD_DIM:])
    wrows = jnp.concatenate([win_state.reshape(nb_, -1, 2 * HEAD_DIM), r["win"][0][:, None, :]], axis=1)
    lw_ = wrows.shape[1]
    dist = (lw_ - 1 - jnp.arange(lw_, dtype=jnp.int32)).astype(F32)
    kw = _rms(wrows[..., :HEAD_DIM], k_norm[2])
    s = jnp.einsum('bhd,bld->bhl', q, kw) - slope[:, None] * dist
    o_win = jnp.einsum('bhl,bld->bhd', jax.nn.softmax(s, axis=-1), wrows[..., HEAD_DIM:])
    g = r["gates"][0][:, :3 * NSA_HEADS].reshape(nb_, 3, NSA_HEADS, 1)
    o = g[:, 0] * o_cmp + g[:, 1] * o_slc + g[:, 2] * o_win
    return o.reshape(nb_, NSA_HEADS * HEAD_DIM), wrows[:, 1:]


def _decode_mla(r, l, cache_mla, page_table, w_ukv, k_gain):
    nb_, n_pages = page_table.shape
    d_qk = MLA_NOPE + MLA_ROPE
    q = r["mla_q"][0].astype(F32).reshape(nb_, MLA_HEADS, LANES)[..., :d_qk]

    def keys(rows):
        lat, kr = rows[..., :MLA_KV_RANK], rows[..., MLA_KV_RANK:]
        kv = (lat @ w_ukv).reshape(lat.shape[:-1] + (MLA_HEADS, MLA_NOPE + MLA_V))
        kr = jnp.broadcast_to(kr[..., None, :], kv.shape[:-1] + (MLA_ROPE,))
        return _rms(jnp.concatenate([kv[..., :MLA_NOPE], kr], axis=-1), k_gain), kv[..., MLA_NOPE:]

    def part(rows):
        k, v = keys(rows)
        s = jnp.einsum('bhd,blhd->bhl', q, k)
        m = s.max(axis=-1)
        e = jnp.exp(s - m[..., None])
        return m, e.sum(axis=-1), jnp.einsum('bhl,blhd->bhd', e, v)

    m_p, l_p, a_p = lax.map(lambda jp: part(cache_mla[l, page_table[:, jp]]), jnp.arange(n_pages))
    m_s, l_s, a_s = part(r["mla_rows"][0][:, None, :])
    m = jnp.concatenate([m_p, m_s[None]], axis=0)
    ls = jnp.concatenate([l_p, l_s[None]], axis=0)
    acc = jnp.concatenate([a_p, a_s[None]], axis=0)
    w = jnp.exp(m - m.max(axis=0))
    o = (acc * w[..., None]).sum(axis=0) / (ls * w).sum(axis=0)[..., None]
    return o.reshape(nb_, MLA_HEADS * MLA_V)


def _decode_sb(r, l, cache_sb, page_table):
    nb_ = page_table.shape[0]
    g = SB_HEADS // SB_KV
    q = r["sb_q"][0].astype(F32).reshape(nb_, SB_KV, g, HEAD_DIM)
    past = cache_sb[l][page_table].reshape(nb_, PAST_LEN, 2, SB_KV, HEAD_DIM)
    z = jnp.einsum('bkgd,blkd->bkgl', q, past[:, :, 0])
    sp = jax.nn.softplus(z)
    a = jnp.exp(z - lax.cumsum(sp, axis=3, reverse=True))
    o = jnp.einsum('bkgl,blkd->bkgd', a, past[:, :, 1])
    return o.reshape(nb_, SB_HEADS * HEAD_DIM)


def sample_layer(y, c, ada_w, ada_b, lw, l, cache_cmp, cache_slc, win_state, cache_mla, cache_sb,
                 page_table, phi_k, phi_v, nsa_k_norm, mla_w_ukv, mla_k_norm, rope_tab):
    nb_ = y.shape[0]
    flat = lambda a: a.reshape(1, nb_, D_MODEL)
    sh1, sc1, g1, sh2, sc2, g2 = (flat(a) for a in _ada(c, ada_w, ada_b))
    ys = flat(y)
    r = in_proj(ys, sh1, sc1, lw, rope_tab, nb_)
    o_nsa, win_new = _decode_nsa(r, l, cache_cmp, cache_slc, win_state, page_table, phi_k, phi_v, nsa_k_norm)
    o_mla = _decode_mla(r, l, cache_mla, page_table, mla_w_ukv, mla_k_norm)
    o_sb = _decode_sb(r, l, cache_sb, page_table)
    ys = merge_block(ys, sh1, sc1, g1, lw, o_nsa[None], o_mla[None], o_sb[None], nb_)
    ys = ffn_block(ys, sh2, sc2, g2, lw, nb_)
    rows5 = lambda a: a.reshape(nb_, -1, 2, 1, HEAD_DIM)
    caches = (rows5(r["cmp"][0]), rows5(r["slc"][0]), rows5(win_new),
              r["mla_rows"][0][:, None, :], r["sb_rows"][0].reshape(nb_, 1, 2, SB_KV, HEAD_DIM))
    return ys.reshape(y.shape), caches


def kernel(x_prompt, x_sample, cache_nsa_cmp, cache_nsa_slc, state_nsa_win, cache_mla, cache_sb,
           page_table, c_prompt, c_sample, ada_w, ada_b, norm_mix, norm_ffn, w_in,
           nsa_q_norm, nsa_k_norm, nsa_phi_k, nsa_phi_v, mla_qa_norm, mla_kva_norm,
           mla_w_uq, mla_w_ukv, mla_q_norm, mla_k_norm, w_branch, w_out, ffn_w_up, ffn_w_down):
    S = x_prompt.shape[1]
    nb_, T = x_sample.shape[:2]
    assert T == 1, "the sample group decodes one token per sequence"
    rope_p = _rope_tables(jnp.arange(S, dtype=jnp.int32))
    rope_s = _rope_tables(jnp.full((nb_,), PAST_LEN, jnp.int32))
    y_p, y_s = x_prompt, x_sample
    outs_p, outs_s = [], []
    for l in range(DEPTH):
        lw = _layer_weights(l, norm_mix, norm_ffn, w_in, nsa_q_norm, nsa_k_norm, mla_qa_norm,
                            mla_kva_norm, mla_w_uq, mla_w_ukv, mla_q_norm, mla_k_norm, w_branch,
                            w_out, ffn_w_up, ffn_w_down)
        y_p, cp = prompt_layer(y_p, c_prompt, ada_w[l], ada_b[l], lw, nsa_phi_k[l], nsa_phi_v[l],
                               nsa_k_norm[l], rope_p)
        y_s, cs = sample_layer(y_s, c_sample, ada_w[l], ada_b[l], lw, l, cache_nsa_cmp, cache_nsa_slc,
                               state_nsa_win[l], cache_mla, cache_sb, page_table, nsa_phi_k[l],
                               nsa_phi_v[l], nsa_k_norm[l], mla_w_ukv[l], mla_k_norm[l], rope_s)
        outs_p.append(cp)
        outs_s.append(cs)
    stk = lambda outs, i: jnp.stack([o[i] for o in outs])
    res = [y_p, y_s]
    for i in range(5):
        res += [stk(outs_p, i), stk(outs_s, i)]
    return tuple(res)
```

```python
import functools

import jax
import jax.numpy as jnp
import numpy as np
from jax import lax
from jax.experimental import pallas as pl
from jax.experimental.pallas import tpu as pltpu

D_MODEL = 1024
DEPTH = 2
PAGE_SIZE = 128
HEAD_DIM = 64
NSA_HEADS = 8
CMP_BLOCK = 64
SEL_BLOCK = 64
SEL_TOP_K = 16
WINDOW = 512
MLA_HEADS = 8
MLA_Q_RANK = 256
MLA_KV_RANK = 128
MLA_NOPE = 64
MLA_ROPE = 32
MLA_V = 64
ROPE_THETA = 10000.0
SB_HEADS = 8
SB_KV = 2
N_BRANCH = 3
BRANCH_WIDTH = 512
D_FF = ((8 * D_MODEL + 3 * 256 - 1) // (3 * 256)) * 256
EPS = 1e-6
NEG = -1e30
FORCED_SCORE = 1e4
NSA_SCALE = HEAD_DIM ** -0.5
MLA_SCALE = (MLA_NOPE + MLA_ROPE) ** -0.5
SB_SCALE = HEAD_DIM ** -0.5
IN_SIZES = (512, 384, 24, 256, 128, 32, 512, 256, 3072)
IN_OFFS = tuple(int(v) for v in np.cumsum((0,) + IN_SIZES))

LANES = 128
VMEM_LIMIT = 56 * 1024 * 1024
BF16 = jnp.bfloat16
F32 = jnp.float32


def _cparams(n_axes):
    return pltpu.CompilerParams(dimension_semantics=("parallel",) * n_axes,
                                vmem_limit_bytes=VMEM_LIMIT)


def _resident(shape):
    zeros = (0,) * len(shape)
    return pl.BlockSpec(shape, lambda *_: zeros, pipeline_mode=pl.Buffered(1))


def _lane_lt64(shape):
    return lax.broadcasted_iota(jnp.int32, shape, len(shape) - 1) % LANES < 64


def _dot(a, b):
    return jnp.dot(a, b, preferred_element_type=F32)


def _dot_t(a, b):
    return lax.dot_general(a, b, (((1,), (1,)), ((), ())), preferred_element_type=F32)


def _modulated(x, g, shift, scale):
    y = x * lax.rsqrt(jnp.mean(x * x, axis=-1, keepdims=True) + EPS) * g
    return y * (1.0 + scale) + shift


def _rope128(x, c, s1, s2):
    return x * c + pltpu.roll(x, LANES - 16, 1) * s1 + pltpu.roll(x, 16, 1) * s2


def _dup_halves(rows):
    r = pltpu.roll(rows, 64, 1)
    lt = _lane_lt64(rows.shape)
    return jnp.where(lt, rows, r), jnp.where(lt, r, rows)


def _in_proj_kernel(x_ref, mod_ref, g_ref, wp_ref, seg_ref, nqg_ref, nkg_ref, qag_ref, kvag_ref,
                    wuq_ref, wuk_ref, wuv_ref, mqg_ref, mkg_ref, rope_ref,
                    cmp_ref, slc_ref, win_ref, sbrow_ref, mlarow_ref, gate_ref,
                    nq_ref, slck_ref, slcv_ref, wink_ref, winv_ref,
                    mq_ref, mk_ref, mv_ref, sq_ref, sbk_ref, sbv_ref):
    x = x_ref[0]
    h = _modulated(x, g_ref[...], mod_ref[0, 0], mod_ref[0, 1]).astype(BF16)
    z = _dot(h, wp_ref[...])
    nq, nkv = z[:, 0:512], z[:, 512:896]
    cq, ckv, krp = z[:, 896:1152], z[:, 1152:1280], z[:, 1280:1408]
    sq, skv, ng = z[:, 1408:1920], z[:, 1920:2176], z[:, 2176:2304]
    cmp_ref[0] = nkv[:, 0:128]
    slc_ref[0] = nkv[:, 128:256]
    win_ref[0] = nkv[:, 256:384]
    sbrow_ref[0] = skv
    gate_ref[0] = jax.nn.sigmoid(ng)
    ss = _dot((nq * nq).astype(BF16), seg_ref[...])
    nq_ref[0] = (nq * lax.rsqrt(ss * (1.0 / HEAD_DIM) + EPS) * nqg_ref[...]).astype(BF16)
    for rows, kg, k_out, v_out in ((nkv[:, 128:256], nkg_ref[1:2], slck_ref, slcv_ref),
                                   (nkv[:, 256:384], nkg_ref[2:3], wink_ref, winv_ref)):
        k2, v2 = _dup_halves(rows)
        kss = jnp.sum(k2 * k2, axis=-1, keepdims=True) * (1.0 / LANES)
        k_out[0] = (k2 * lax.rsqrt(kss + EPS) * kg).astype(BF16)
        v_out[0] = v2.astype(BF16)
    sq_ref[0] = (sq * SB_SCALE).astype(BF16)
    k01, v01 = skv[:, 0:128], skv[:, 128:256]
    ka, kb = _dup_halves(k01)
    va, vb = _dup_halves(v01)
    sbk_ref[0] = jnp.concatenate([ka, kb], axis=1).astype(BF16)
    sbv_ref[0] = jnp.concatenate([va, vb], axis=1).astype(BF16)
    rc, rs1, rs2 = rope_ref[0], rope_ref[1], rope_ref[2]
    lat = ckv * lax.rsqrt(jnp.mean(ckv * ckv, axis=-1, keepdims=True) + EPS) * kvag_ref[...]
    kr = _rope128(krp, rc, rs1, rs2)
    mlarow_ref[0, :, 0:128] = lat
    mlarow_ref[0, :, 128:160] = kr[:, 64:96]
    latb = lat.astype(BF16)
    kn = _dot(latb, wuk_ref[...])
    mv_ref[0] = _dot(latb, wuv_ref[...]).astype(BF16)
    cqn = (cq * lax.rsqrt(jnp.mean(cq * cq, axis=-1, keepdims=True) + EPS) * qag_ref[...]).astype(BF16)
    qf = _dot(cqn, wuq_ref[...])
    inv_d = 1.0 / (MLA_NOPE + MLA_ROPE)
    for hd in range(MLA_HEADS):
        sl = slice(hd * LANES, (hd + 1) * LANES)
        kh = kn[:, sl] + kr
        kss = jnp.sum(kh * kh, axis=-1, keepdims=True) * inv_d
        mk_ref[0, :, sl] = (kh * lax.rsqrt(kss + EPS) * mkg_ref[...]).astype(BF16)
        qh = _rope128(qf[:, sl], rc, rs1, rs2)
        qss = jnp.sum(qh * qh, axis=-1, keepdims=True) * inv_d
        mq_ref[0, :, sl] = (qh * lax.rsqrt(qss + EPS) * mqg_ref[...]).astype(BF16)


def _pad_heads(w, n_heads, parts, width=LANES):
    k = w.shape[0]
    w = w.reshape(k, n_heads, sum(parts))
    return jnp.pad(w, ((0, 0), (0, 0), (0, width - sum(parts)))).reshape(k, n_heads * width)


def _rope_tables(pos):
    half = MLA_ROPE // 2
    freq = ROPE_THETA ** (-jnp.arange(half, dtype=F32) / half)
    ang = pos.astype(F32)[:, None] * freq
    cos, sin = jnp.cos(ang), jnp.sin(ang)
    t = pos.shape[0]
    one, zero = jnp.ones((t, 64), F32), jnp.zeros((t, 64), F32)
    z16, z32 = jnp.zeros((t, 16), F32), jnp.zeros((t, 32), F32)
    c = jnp.concatenate([one, cos, cos, z32], axis=1)
    s1 = jnp.concatenate([zero, -sin, z16, z32], axis=1)
    s2 = jnp.concatenate([zero, z16, sin, z32], axis=1)
    return jnp.stack([c, s1, s2])


def _layer_weights(l, norm_mix, norm_ffn, w_in, nsa_q_norm, nsa_k_norm, mla_qa_norm, mla_kva_norm,
                   mla_w_uq, mla_w_ukv, mla_q_norm, mla_k_norm, w_branch, w_out, ffn_w_up, ffn_w_down):
    o = IN_OFFS
    w = w_in[l]
    col = lambda i: w[:, o[i]:o[i + 1]]
    zpad = lambda n: jnp.zeros((D_MODEL, n), F32)
    wp = jnp.concatenate([col(0), col(1), col(3), col(4), zpad(64), col(5), zpad(32),
                          col(6), col(7), col(2), zpad(LANES - 24)], axis=1).astype(BF16)
    seg = jnp.asarray(np.kron(np.eye(NSA_HEADS), np.ones((HEAD_DIM, HEAD_DIM))), BF16)
    d_qk = MLA_NOPE + MLA_ROPE
    wuq = _pad_heads(mla_w_uq[l], MLA_HEADS, (d_qk,)).astype(BF16)
    wukv = mla_w_ukv[l].reshape(MLA_KV_RANK, MLA_HEADS, MLA_NOPE + MLA_V)
    wuk = _pad_heads(wukv[:, :, :MLA_NOPE].reshape(MLA_KV_RANK, -1), MLA_HEADS, (MLA_NOPE,)).astype(BF16)
    wuv = wukv[:, :, MLA_NOPE:].reshape(MLA_KV_RANK, -1).astype(BF16)
    pad_g = lambda g: jnp.pad(g, (0, LANES - d_qk)).reshape(1, LANES)
    return dict(
        g_mix=norm_mix[l].reshape(1, D_MODEL), g_ffn=norm_ffn[l].reshape(1, D_MODEL),
        wp=wp, seg=seg,
        nqg=(jnp.tile(nsa_q_norm[l], NSA_HEADS) * NSA_SCALE).reshape(1, 512),
        nkg=jnp.tile(nsa_k_norm[l], (1, 2)),
        qag=mla_qa_norm[l].reshape(1, MLA_Q_RANK), kvag=mla_kva_norm[l].reshape(1, MLA_KV_RANK),
        wuq=wuq, wuk=wuk, wuv=wuv,
        mqg=pad_g(mla_q_norm[l]) * MLA_SCALE, mkg=pad_g(mla_k_norm[l]),
        wmg=w[:, o[8]:o[9]].astype(BF16),
        wbr=w_branch[l].astype(BF16), wout=w_out[l].astype(BF16),
        wup=ffn_w_up[l].astype(BF16), wdn=ffn_w_down[l].astype(BF16))


def in_proj(x, shift, scale, lw, rope_tab, bm):
    B, T, D = x.shape
    R = shift.shape[1]
    mod = jnp.stack([shift, scale], axis=1)
    rb = 1 if R == 1 else bm
    mod_map = (lambda b, i: (b, 0, 0, 0)) if R == 1 else (lambda b, i: (b, 0, i, 0))
    row = lambda w: pl.BlockSpec((1, bm, w), lambda b, i: (b, i, 0))
    outs = [("cmp", 128, F32), ("slc", 128, F32), ("win", 128, F32), ("sb_rows", 256, F32),
            ("mla_rows", MLA_KV_RANK + MLA_ROPE, F32), ("gates", 128, F32),
            ("nq", 512, BF16), ("slc_k", 128, BF16), ("slc_v", 128, BF16),
            ("win_k", 128, BF16), ("win_v", 128, BF16),
            ("mla_q", 1024, BF16), ("mla_k", 1024, BF16), ("mla_v", 512, BF16),
            ("sb_q", 512, BF16), ("sb_k", 256, BF16), ("sb_v", 256, BF16)]
    res = pl.pallas_call(
        _in_proj_kernel,
        grid=(B, T // bm),
        in_specs=[row(D), pl.BlockSpec((1, 2, rb, D), mod_map), _resident((1, D)),
                  _resident(lw["wp"].shape), _resident(lw["seg"].shape), _resident((1, 512)),
                  _resident((3, 128)), _resident((1, MLA_Q_RANK)), _resident((1, MLA_KV_RANK)),
                  _resident(lw["wuq"].shape), _resident(lw["wuk"].shape), _resident(lw["wuv"].shape),
                  _resident((1, LANES)), _resident((1, LANES)),
                  pl.BlockSpec((3, bm, LANES), lambda b, i: (0, i, 0))],
        out_specs=[row(w) for _, w, _ in outs],
        out_shape=[jax.ShapeDtypeStruct((B, T, w), dt) for _, w, dt in outs],
        compiler_params=_cparams(2),
        name="in_proj",
    )(x, mod, lw["g_mix"], lw["wp"], lw["seg"], lw["nqg"], lw["nkg"], lw["qag"], lw["kvag"],
      lw["wuq"], lw["wuk"], lw["wuv"], lw["mqg"], lw["mkg"], rope_tab)
    return {name: r for (name, _, _), r in zip(outs, res)}


def _softmax_step(s, v, m, l, acc):
    m_new = jnp.maximum(m, jnp.max(s, axis=-1, keepdims=True))
    alpha = jnp.exp(m - m_new)
    p = jnp.exp(s - m_new)
    l_new = alpha * l + jnp.sum(p, axis=-1, keepdims=True)
    pv = _dot(p.reshape(-1, p.shape[-1]).astype(BF16), v).reshape(acc.shape)
    return m_new, l_new, alpha * acc + pv


def _mla_attn_kernel(q_ref, k_ref, v_ref, o_ref, *, bq):
    i = pl.program_id(2)
    row = lax.broadcasted_iota(jnp.int32, (bq, bq), 0)
    col = lax.broadcasted_iota(jnp.int32, (bq, bq), 1)
    outs = []
    for hh in range(2):
        lanes = slice(hh * LANES, (hh + 1) * LANES)
        q = q_ref[0, :, lanes]

        def tile(t, carry, diagonal):
            k = k_ref[0, pl.ds(t * bq, bq), lanes]
            v = v_ref[0, pl.ds(t * bq, bq), :]
            s = _dot_t(q, k)
            if diagonal:
                s = jnp.where(col <= row, s, NEG)
            return _softmax_step(s, v, *carry)

        init = (jnp.full((bq, 1), NEG, F32), jnp.zeros((bq, 1), F32), jnp.zeros((bq, LANES), F32))
        carry = lax.fori_loop(0, i, lambda t, c: tile(t, c, False), init)
        m, l, acc = tile(i, carry, True)
        outs.append(acc / l)
    o_ref[0] = jnp.where(_lane_lt64((bq, LANES)), outs[0], outs[1])


def mla_attention(q, k, v, bq):
    B, S, _ = q.shape
    return pl.pallas_call(
        functools.partial(_mla_attn_kernel, bq=bq),
        grid=(B, MLA_HEADS // 2, S // bq),
        in_specs=[pl.BlockSpec((1, bq, 2 * LANES), lambda b, j, i: (b, i, j)),
                  pl.BlockSpec((1, S, 2 * LANES), lambda b, j, i: (b, 0, j)),
                  pl.BlockSpec((1, S, LANES), lambda b, j, i: (b, 0, j))],
        out_specs=pl.BlockSpec((1, bq, LANES), lambda b, j, i: (b, i, j)),
        out_shape=jax.ShapeDtypeStruct((B, S, MLA_HEADS * MLA_V), F32),
        compiler_params=_cparams(3),
        name="mla_attention",
    )(q, k, v)


def _softplus(z):
    return jnp.maximum(z, 0.0) + jnp.log1p(jnp.exp(-jnp.abs(z)))


def _sb_attn_kernel(q_ref, k_ref, v_ref, tri_ref, o_ref, *, bq):
    i = pl.program_id(2)
    n = SB_HEADS // SB_KV
    row = lax.broadcasted_iota(jnp.int32, (bq, bq), 0)
    col = lax.broadcasted_iota(jnp.int32, (bq, bq), 1)
    lt = _lane_lt64((bq, LANES))
    qs = []
    for h in range(n):
        qp = q_ref[0, :, (h // 2) * LANES:(h // 2 + 1) * LANES]
        qs.append(jnp.where(lt == (h % 2 == 0), qp, jnp.zeros_like(qp)))
    q = jnp.concatenate(qs, axis=0)
    tri = tri_ref[...]

    def tile(t, carry, diagonal):
        c, acc = carry
        k = k_ref[0, pl.ds(t * bq, bq), :]
        v = v_ref[0, pl.ds(t * bq, bq), :]
        z = _dot_t(q, k).reshape(n, bq, bq)
        sp = _softplus(z)
        if diagonal:
            valid = (col < row)[None]
            sp = jnp.where(valid, sp, 0.0)
        sp2 = sp.reshape(n * bq, bq)
        hi = sp2.astype(BF16)
        lo = (sp2 - hi.astype(F32)).astype(BF16)
        cum = (_dot(hi, tri) + _dot(lo, tri)).reshape(n, bq, bq)
        a = jnp.exp(z - cum - c)
        if diagonal:
            a = jnp.where(valid, a, 0.0)
        acc = acc + _dot(a.reshape(n * bq, bq).astype(BF16), v).reshape(n, bq, LANES)
        return c + cum[:, :, 0:1], acc

    carry = tile(i, (jnp.zeros((n, bq, 1), F32), jnp.zeros((n, bq, LANES), F32)), True)
    _, acc = lax.fori_loop(0, i, lambda u, cr: tile(i - 1 - u, cr, False), carry)
    for j in range(n // 2):
        o_ref[0, :, j * LANES:(j + 1) * LANES] = jnp.where(lt, acc[2 * j], acc[2 * j + 1])


def sb_attention(q, k2, v2, bq):
    B, S, _ = q.shape
    tri = jnp.asarray(np.tril(np.ones((bq, bq), np.float32)), BF16)
    w = (SB_HEADS // SB_KV) * HEAD_DIM
    return pl.pallas_call(
        functools.partial(_sb_attn_kernel, bq=bq),
        grid=(B, SB_KV, S // bq),
        in_specs=[pl.BlockSpec((1, bq, w), lambda b, g, i: (b, i, g)),
                  pl.BlockSpec((1, S, LANES), lambda b, g, i: (b, 0, g)),
                  pl.BlockSpec((1, S, LANES), lambda b, g, i: (b, 0, g)),
                  _resident((bq, bq))],
        out_specs=pl.BlockSpec((1, bq, w), lambda b, g, i: (b, i, g)),
        out_shape=jax.ShapeDtypeStruct((B, S, SB_HEADS * HEAD_DIM), F32),
        compiler_params=_cparams(3),
        name="sb_attention",
    )(q, k2, v2, tri)


def _cmp_blocks_kernel(rows_ref, wk_ref, wv_ref, g_ref, kc_ref, vc_ref):
    s = rows_ref.shape[1]
    nb = s // CMP_BLOCK
    means = rows_ref[0].reshape(nb, CMP_BLOCK, LANES).sum(axis=1) * (1.0 / CMP_BLOCK)
    mb = means.astype(BF16)
    k2 = _dot(mb, wk_ref[...])
    kss = jnp.sum(k2 * k2, axis=-1, keepdims=True) * (1.0 / LANES)
    kc_ref[0] = (k2 * lax.rsqrt(kss + EPS) * g_ref[...]).astype(BF16)
    vc_ref[0] = _dot(mb, wv_ref[...]).astype(BF16)


def cmp_blocks(rows, phi_k, phi_v, k_gain):
    B, S, _ = rows.shape
    nb = S // CMP_BLOCK
    z = jnp.zeros((HEAD_DIM, LANES), F32)
    wk = jnp.concatenate([jnp.tile(phi_k, (1, 2)), z], axis=0).astype(BF16)
    wv = jnp.concatenate([z, jnp.tile(phi_v, (1, 2))], axis=0).astype(BF16)
    return pl.pallas_call(
        _cmp_blocks_kernel,
        grid=(B,),
        in_specs=[pl.BlockSpec((1, S, LANES), lambda b: (b, 0, 0)),
                  _resident((LANES, LANES)), _resident((LANES, LANES)), _resident((1, LANES))],
        out_specs=[pl.BlockSpec((1, nb, LANES), lambda b: (b, 0, 0))] * 2,
        out_shape=[jax.ShapeDtypeStruct((B, nb, LANES), BF16)] * 2,
        compiler_params=_cparams(1),
        name="cmp_blocks",
    )(rows, wk, wv, jnp.tile(k_gain, 2).reshape(1, LANES))


def _nsa_attn_kernel(q_ref, kc_ref, vc_ref, ks_ref, vs_ref, kw_ref, vw_ref, blk_ref, gate_ref,
                     gexp_ref, o_ref, ms_ref, ls_ref, as_ref, mw_ref, lw_ref, aw_ref, *, bq):
    i = pl.program_id(1)
    nh = NSA_HEADS
    nb = kc_ref.shape[1]
    lt = _lane_lt64((bq, LANES))
    qs = []
    for h in range(nh):
        qp = q_ref[0, :, (h // 2) * LANES:(h // 2 + 1) * LANES]
        qs.append(jnp.where(lt == (h % 2 == 0), qp, jnp.zeros_like(qp)))
    q = jnp.concatenate(qs, axis=0)
    hidx = lax.broadcasted_iota(jnp.int32, (nh, 1, 1), 0).astype(F32)
    slope = jnp.exp2(-(hidx + 1.0))

    qpos_c = i * bq + lax.broadcasted_iota(jnp.int32, (bq, nb), 0)
    blk = lax.broadcasted_iota(jnp.int32, (bq, nb), 1)
    dist_c = (qpos_c - ((blk + 1) * CMP_BLOCK - 1)).astype(F32)
    ok_c = dist_c >= 0
    s = _dot_t(q, kc_ref[0]).reshape(nh, bq, nb) - slope * dist_c[None]
    s = jnp.where(ok_c[None], s, NEG)
    e = jnp.exp(s - jnp.max(s, axis=-1, keepdims=True))
    p_cmp = e / jnp.sum(e, axis=-1, keepdims=True) * ok_c[None].astype(F32)
    o_cmp = _dot(p_cmp.reshape(nh * bq, nb).astype(BF16), vc_ref[0]).reshape(nh, bq, LANES)
    cur = qpos_c // SEL_BLOCK
    forced = (blk == 0) | (blk == cur) | (blk == cur - 1)
    score = jnp.where(forced, FORCED_SCORE, jnp.sum(p_cmp, axis=0))
    score = jnp.where(blk <= cur, score, -1.0)
    blkf = blk.astype(F32)
    sel = jnp.zeros((bq, nb), F32)
    for _ in range(SEL_TOP_K):
        mx = jnp.max(score, axis=-1, keepdims=True)
        idx = jnp.min(jnp.where(score == mx, blkf, float(nb)), axis=-1, keepdims=True)
        hit = blkf == idx
        sel = jnp.where(hit & (mx >= 0.0), 1.0, sel)
        score = jnp.where(hit, -2.0, score)
    selb = sel.astype(BF16)

    row = i * bq + lax.broadcasted_iota(jnp.int32, (bq, bq), 0)
    col = lax.broadcasted_iota(jnp.int32, (bq, bq), 1)

    def tile(t, k_ref, v_ref, m_ref, l_ref, a_ref, window):
        k = k_ref[0, pl.ds(t * bq, bq), :]
        v = v_ref[0, pl.ds(t * bq, bq), :]
        dist = (row - (t * bq + col)).astype(F32)
        if window:
            ok = (dist >= 0) & (dist <= WINDOW)
        else:
            picked = _dot(selb, blk_ref[:, pl.ds(t * bq, bq)])
            ok = (dist >= 0) & (picked > 0.5)
        s = _dot_t(q, k).reshape(nh, bq, bq) - slope * dist[None]
        s = jnp.where(ok[None], s, NEG)
        m, l, a = _softmax_step(s, v, m_ref[...], l_ref[...], a_ref[...])
        m_ref[...] = m
        l_ref[...] = l
        a_ref[...] = a

    for m_ref, l_ref, a_ref in ((ms_ref, ls_ref, as_ref), (mw_ref, lw_ref, aw_ref)):
        m_ref[...] = jnp.full(m_ref.shape, NEG, F32)
        l_ref[...] = jnp.zeros(l_ref.shape, F32)
        a_ref[...] = jnp.zeros(a_ref.shape, F32)

    def slc_tile(t, carry):
        tile(t, ks_ref, vs_ref, ms_ref, ls_ref, as_ref, False)
        return carry

    def win_tile(t, carry):
        tile(t, kw_ref, vw_ref, mw_ref, lw_ref, aw_ref, True)
        return carry

    lax.fori_loop(0, i + 1, slc_tile, 0)
    lax.fori_loop(jnp.maximum(i - pl.cdiv(WINDOW, bq), 0), i + 1, win_tile, 0)

    g = gate_ref[0]
    ghi = g.astype(BF16)
    glo = (g - ghi.astype(F32)).astype(BF16)
    gx = _dot(ghi, gexp_ref[...]) + _dot(glo, gexp_ref[...])
    o_slc = as_ref[...] / ls_ref[...]
    o_win = aw_ref[...] / lw_ref[...]
    w = NSA_HEADS * HEAD_DIM
    for j in range(nh // 2):
        lanes = slice(j * LANES, (j + 1) * LANES)
        pair = lambda o: jnp.where(lt, o[2 * j], o[2 * j + 1])
        o_ref[0, :, lanes] = (gx[:, lanes] * pair(o_cmp)
                              + gx[:, w + j * LANES:w + (j + 1) * LANES] * pair(o_slc)
                              + gx[:, 2 * w + j * LANES:2 * w + (j + 1) * LANES] * pair(o_win))


def nsa_attention(q, kc2, vc2, slc_k, slc_v, win_k, win_v, gates, bq):
    B, S, _ = q.shape
    nb = S // SEL_BLOCK
    blk_of_key = np.arange(S)[None, :] // SEL_BLOCK == np.arange(nb)[:, None]
    blk_exp = jnp.asarray(blk_of_key.astype(np.float32), BF16)
    gexp = np.zeros((LANES, 3 * 512), np.float32)
    for t in range(3):
        for h in range(NSA_HEADS):
            gexp[t * 8 + h, t * 512 + h * 64:t * 512 + (h + 1) * 64] = 1.0
    tile_spec = pl.BlockSpec((1, bq, LANES), lambda b, i: (b, i, 0))
    seq = lambda n: pl.BlockSpec((1, n, LANES), lambda b, i: (b, 0, 0))
    nh = NSA_HEADS
    return pl.pallas_call(
        functools.partial(_nsa_attn_kernel, bq=bq),
        grid=(B, S // bq),
        in_specs=[pl.BlockSpec((1, bq, 512), lambda b, i: (b, i, 0)), seq(nb), seq(nb),
                  seq(S), seq(S), seq(S), seq(S), _resident((nb, S)), tile_spec,
                  _resident((LANES, 3 * 512))],
        out_specs=pl.BlockSpec((1, bq, 512), lambda b, i: (b, i, 0)),
        out_shape=jax.ShapeDtypeStruct((B, S, 512), F32),
        scratch_shapes=[pltpu.VMEM((nh, bq, 1), F32), pltpu.VMEM((nh, bq, 1), F32),
                        pltpu.VMEM((nh, bq, LANES), F32)] * 2,
        compiler_params=_cparams(2),
        name="nsa_attention",
    )(q, kc2, vc2, slc_k, slc_v, win_k, win_v, blk_exp, gates, jnp.asarray(gexp, BF16))


def _merge_kernel(x_ref, mod_ref, g_ref, wmg_ref, on_ref, om_ref, os_ref, wbr_ref, wout_ref, o_ref):
    x = x_ref[0]
    h = _modulated(x, g_ref[...], mod_ref[0, 0], mod_ref[0, 1]).astype(BF16)
    d = x.shape[-1]
    merged = jnp.zeros(x.shape, F32)
    for n, br_ref in enumerate((on_ref, om_ref, os_ref)):
        gate = jax.nn.sigmoid(_dot(h, wmg_ref[:, n * d:(n + 1) * d]))
        merged = merged + gate * _dot(br_ref[0].astype(BF16), wbr_ref[n])
    o_ref[0] = x + mod_ref[0, 2] * _dot(merged.astype(BF16), wout_ref[...])


def _mod_specs(mods, bm):
    R = mods.shape[2]
    rb = 1 if R == 1 else bm
    mod_map = (lambda b, i: (b, 0, 0, 0)) if R == 1 else (lambda b, i: (b, 0, i, 0))
    return pl.BlockSpec((1, mods.shape[1], rb, mods.shape[3]), mod_map)


def merge_block(x, shift, scale, gate, lw, o_nsa, o_mla, o_sb, bm):
    B, T, D = x.shape
    mod = jnp.stack([shift, scale, gate], axis=1)
    row = lambda w: pl.BlockSpec((1, bm, w), lambda b, i: (b, i, 0))
    return pl.pallas_call(
        _merge_kernel,
        grid=(B, T // bm),
        in_specs=[row(D), _mod_specs(mod, bm), _resident((1, D)), _resident(lw["wmg"].shape),
                  row(BRANCH_WIDTH), row(BRANCH_WIDTH), row(BRANCH_WIDTH),
                  _resident(lw["wbr"].shape), _resident(lw["wout"].shape)],
        out_specs=row(D),
        out_shape=jax.ShapeDtypeStruct(x.shape, F32),
        compiler_params=_cparams(2),
        name="merge_block",
    )(x, mod, lw["g_mix"], lw["wmg"], o_nsa, o_mla, o_sb, lw["wbr"], lw["wout"])


FFN_CHUNKS = 4


def _ffn_kernel(x_ref, mod_ref, g_ref, wup_ref, wdn_ref, o_ref):
    x = x_ref[0]
    h = _modulated(x, g_ref[...], mod_ref[0, 0], mod_ref[0, 1]).astype(BF16)
    dff = wdn_ref.shape[0]
    ck = dff // FFN_CHUNKS
    acc = jnp.zeros(x.shape, F32)
    for c in range(FFN_CHUNKS):
        a = _dot(h, wup_ref[:, c * ck:(c + 1) * ck])
        b = _dot(h, wup_ref[:, dff + c * ck:dff + (c + 1) * ck])
        acc = acc + _dot((a * jax.nn.sigmoid(a) * b).astype(BF16), wdn_ref[c * ck:(c + 1) * ck, :])
    o_ref[0] = x + mod_ref[0, 2] * acc


def ffn_block(x, shift, scale, gate, lw, bm):
    B, T, D = x.shape
    mod = jnp.stack([shift, scale, gate], axis=1)
    row = pl.BlockSpec((1, bm, D), lambda b, i: (b, i, 0))
    return pl.pallas_call(
        _ffn_kernel,
        grid=(B, T // bm),
        in_specs=[row, _mod_specs(mod, bm), _resident((1, D)),
                  _resident(lw["wup"].shape), _resident(lw["wdn"].shape)],
        out_specs=row,
        out_shape=jax.ShapeDtypeStruct(x.shape, F32),
        compiler_params=_cparams(2),
        name="ffn_block",
    )(x, mod, lw["g_ffn"], lw["wup"], lw["wdn"])


def _ada(c, w, b):
    mod = jax.nn.silu(c) @ w + b
    return jnp.split(mod[:, None, :], 6, axis=-1)


def prompt_layer(y, c, ada_w, ada_b, lw, phi_k, phi_v, nsa_k_norm, rope_tab, bm=256, bq=256):
    B, S, _ = y.shape
    sh1, sc1, g1, sh2, sc2, g2 = _ada(c, ada_w, ada_b)
    r = in_proj(y, sh1, sc1, lw, rope_tab, bm)
    kc2, vc2 = cmp_blocks(r["cmp"], phi_k, phi_v, nsa_k_norm[0])
    o_nsa = nsa_attention(r["nq"], kc2, vc2, r["slc_k"], r["slc_v"], r["win_k"], r["win_v"],
                          r["gates"], bq)
    o_mla = mla_attention(r["mla_q"], r["mla_k"], r["mla_v"], bq)
    o_sb = sb_attention(r["sb_q"], r["sb_k"], r["sb_v"], bq)
    y = merge_block(y, sh1, sc1, g1, lw, o_nsa, o_mla, o_sb, bm)
    y = ffn_block(y, sh2, sc2, g2, lw, bm)
    rows5 = lambda a: a.reshape(B, S, 2, 1, HEAD_DIM)
    caches = (rows5(r["cmp"]), rows5(r["slc"]), rows5(r["win"])[:, max(S - WINDOW, 0):],
              r["mla_rows"], r["sb_rows"].reshape(B, S, 2, SB_KV, HEAD_DIM))
    return y, caches


PAGES_PER_CHUNK = 16
CHUNK_ROWS = PAGES_PER_CHUNK * PAGE_SIZE
DEC_ROWS = 16


def _stream_pages(pt_ref, cache_ref, l, buf_ref, sem_ref, n_chunks, reverse, body, init):
    b = pl.program_id(0)
    total = pl.num_programs(0) * n_chunks

    def copies(n, slot):
        seq, k = n // n_chunks, n % n_chunks
        c = (n_chunks - 1 - k) if reverse else k
        return [pltpu.make_async_copy(cache_ref.at[l, pt_ref[seq, c * PAGES_PER_CHUNK + p]],
                                      buf_ref.at[slot, pl.ds(p * PAGE_SIZE, PAGE_SIZE)],
                                      sem_ref.at[slot]) for p in range(PAGES_PER_CHUNK)]

    @pl.when(b == 0)
    def _():
        for cp in copies(0, 0):
            cp.start()

    def step(k, carry):
        n = b * n_chunks + k
        slot = n % 2

        @pl.when(n + 1 < total)
        def _():
            for cp in copies(n + 1, 1 - slot):
                cp.start()

        for cp in copies(n, slot):
            cp.wait()
        c = (n_chunks - 1 - k) if reverse else k
        return body(c, buf_ref[slot], carry)

    return lax.fori_loop(0, n_chunks, step, init)


def _paged_call(kern, name, page_table, operands, in_specs, cache, out_shapes, out_specs, scratch,
                extra_prefetch=()):
    nseq = page_table.shape[0]
    return pl.pallas_call(
        kern,
        grid_spec=pltpu.PrefetchScalarGridSpec(
            num_scalar_prefetch=1 + len(extra_prefetch),
            grid=(nseq,),
            in_specs=list(in_specs) + [pl.BlockSpec(memory_space=pl.ANY)],
            out_specs=out_specs,
            scratch_shapes=list(scratch) + [pltpu.SemaphoreType.DMA((2,))]),
        out_shape=out_shapes,
        compiler_params=pltpu.CompilerParams(dimension_semantics=("arbitrary",),
                                             vmem_limit_bytes=VMEM_LIMIT),
        name=name,
    )(page_table, *extra_prefetch, *operands, cache)


def _seq_spec(shape):
    zeros = (0,) * (len(shape) - 1)
    return pl.BlockSpec((1,) + tuple(shape[1:]), lambda b, *_: (b,) + zeros)


def _const_spec(shape):
    zeros = (0,) * len(shape)
    return pl.BlockSpec(tuple(shape), lambda b, *_: zeros)


def _head_slopes(n):
    h = lax.broadcasted_iota(jnp.int32, (n, 1), 0).astype(F32)
    return jnp.exp2(-(h + 1.0))


def _norm_k_half(rows, gain):
    lt = _lane_lt64(rows.shape)
    kss = jnp.sum(jnp.where(lt, rows * rows, 0.0), axis=-1, keepdims=True) * (1.0 / HEAD_DIM)
    return jnp.where(lt, rows * lax.rsqrt(kss + EPS) * gain, rows)


def _nsa_cmp_decode_kernel(pt_ref, q_ref, new_ref, wblk_ref, g_ref, cache_ref, ocmp_ref, idx_ref,
                           means_ref, buf_ref, sem_ref, *, l, n_chunks):
    blocks_per_chunk = CHUNK_ROWS // CMP_BLOCK
    nb_past = n_chunks * blocks_per_chunk
    past_len = nb_past * CMP_BLOCK
    nbp = means_ref.shape[0]

    def body(c, rows, carry):
        sums = rows.reshape(blocks_per_chunk, CMP_BLOCK, LANES).sum(axis=1) * (1.0 / CMP_BLOCK)
        means_ref[pl.ds(pl.multiple_of(c * blocks_per_chunk, blocks_per_chunk), blocks_per_chunk), :] = sums
        return carry

    _stream_pages(pt_ref, cache_ref, l, buf_ref, sem_ref, n_chunks, False, body, 0)
    means_ref[nb_past:nb_past + 8, :] = new_ref[0] * (1.0 / CMP_BLOCK)
    means_ref[nb_past + 8:, :] = jnp.zeros((nbp - nb_past - 8, LANES), F32)
    kcv = _dot(means_ref[...].astype(BF16), wblk_ref[...])
    kn = _norm_k_half(kcv, g_ref[...]).astype(BF16)
    s = _dot_t(q_ref[0], kn)[0:NSA_HEADS]
    j = lax.broadcasted_iota(jnp.int32, (1, nbp), 1)
    dist = (past_len - ((j + 1) * CMP_BLOCK - 1)).astype(F32)
    ok = dist >= 0
    s = jnp.where(ok, s - _head_slopes(NSA_HEADS) * dist, NEG)
    e = jnp.exp(s - jnp.max(s, axis=-1, keepdims=True))
    p = e / jnp.sum(e, axis=-1, keepdims=True) * ok.astype(F32)
    ocmp_ref[0] = _dot(p.astype(BF16), kn)
    cur = past_len // SEL_BLOCK
    forced = (j == 0) | (j == cur) | (j == cur - 1)
    score = jnp.where(forced, FORCED_SCORE, jnp.sum(p, axis=0, keepdims=True))
    score = jnp.where(j <= cur, score, -1.0)
    jf = j.astype(F32)
    lane = lax.broadcasted_iota(jnp.int32, (1, LANES), 1)
    picks = jnp.full((1, LANES), -1.0, F32)
    for it in range(SEL_TOP_K):
        mx = jnp.max(score, axis=-1, keepdims=True)
        idx = jnp.min(jnp.where(score == mx, jf, float(nbp)), axis=-1, keepdims=True)
        picks = jnp.where(lane == it, jnp.where(mx >= 0.0, idx, -1.0), picks)
        score = jnp.where(jf == idx, -2.0, score)
    idx_ref[0] = picks.astype(jnp.int32)


def _nsa_sel_decode_kernel(pt_ref, idx_ref, q_ref, new_ref, win_ref, ocmp_ref, gate_ref, g_ref,
                           cache_ref, o_ref, buf_ref, sem_ref, *, l, nb_past):
    b = pl.program_id(0)
    past_len = nb_past * SEL_BLOCK
    per_page = PAGE_SIZE // SEL_BLOCK

    def copies(seq, slot):
        cps = []
        for jj in range(SEL_TOP_K):
            ix = jnp.clip(idx_ref[seq, jj], 0, nb_past - 1)
            src = cache_ref.at[l, pt_ref[seq, ix // per_page], pl.ds((ix % per_page) * SEL_BLOCK, SEL_BLOCK)]
            cps.append(pltpu.make_async_copy(src, buf_ref.at[slot, pl.ds(jj * SEL_BLOCK, SEL_BLOCK)],
                                             sem_ref.at[slot]))
        return cps

    @pl.when(b == 0)
    def _():
        for cp in copies(0, 0):
            cp.start()

    slot = b % 2

    @pl.when(b + 1 < pl.num_programs(0))
    def _():
        for cp in copies(b + 1, 1 - slot):
            cp.start()

    for cp in copies(b, slot):
        cp.wait()

    q = q_ref[0]
    qf = q[0:NSA_HEADS].astype(F32)
    slope = _head_slopes(NSA_HEADS)

    def attend(rows, gain, dist, valid, new_row, new_valid):
        rn = _norm_k_half(rows, gain).astype(BF16)
        s = _dot_t(q, rn)[0:NSA_HEADS] - slope * dist
        if valid is not None:
            s = jnp.where(valid, s, NEG)
        nn = _norm_k_half(new_row, gain).astype(BF16).astype(F32)
        s_new = jnp.where(new_valid, jnp.sum(qf * nn, axis=-1, keepdims=True), NEG)
        m = jnp.maximum(jnp.max(s, axis=-1, keepdims=True), s_new)
        p = jnp.exp(s - m)
        p_new = jnp.exp(s_new - m)
        den = jnp.sum(p, axis=-1, keepdims=True) + p_new
        return (_dot(p.astype(BF16), rn) + p_new * new_row) / den

    n_sel = SEL_TOP_K * SEL_BLOCK
    lane = lax.broadcasted_iota(jnp.int32, (1, n_sel), 1)
    blk = jnp.full((1, n_sel), -1, jnp.int32)
    has_new = False
    for jj in range(SEL_TOP_K):
        ix = idx_ref[b, jj]
        blk = jnp.where(lane // SEL_BLOCK == jj, ix, blk)
        has_new = jnp.logical_or(has_new, ix == nb_past)
    valid = (blk >= 0) & (blk < nb_past)
    dist = (past_len - (blk * SEL_BLOCK + lane % SEL_BLOCK)).astype(F32)
    o_slc = attend(buf_ref[slot], g_ref[0:1], dist, valid, new_ref[0, 0:1], has_new)
    lw = win_ref.shape[1]
    dist_w = (lw - lax.broadcasted_iota(jnp.int32, (1, lw), 1)).astype(F32)
    o_win = attend(win_ref[0], g_ref[1:2], dist_w, None, new_ref[0, 1:2], True)
    gt = gate_ref[0]
    o_ref[0] = gt[:, 0:1] * ocmp_ref[0] + gt[:, 1:2] * o_slc + gt[:, 2:3] * o_win


def _mla_absorb_kernel(q_ref, gk_ref, wukt_ref, o_ref):
    for h in range(MLA_HEADS):
        qg = q_ref[:, h * LANES:(h + 1) * LANES].astype(F32) * gk_ref[...]
        o_ref[h, :, 0:MLA_KV_RANK] = _dot(qg.astype(BF16), wukt_ref[h]).astype(BF16)
        o_ref[h, :, MLA_KV_RANK:] = qg[:, MLA_NOPE:MLA_NOPE + MLA_ROPE].astype(BF16)


def _mla_decode_kernel(pt_ref, q_ref, wk_ref, cache_ref, acc_ref, ml_ref, lhs_ref, buf_ref, sem_ref,
                       *, l, n_chunks):
    b = pl.program_id(0)

    @pl.when(b == 0)
    def _():
        lhs_ref[DEC_ROWS:, :] = wk_ref[...]

    lhs_ref[0:DEC_ROWS, :] = q_ref[0]
    n_k = MLA_HEADS * MLA_NOPE
    inv_d = 1.0 / (MLA_NOPE + MLA_ROPE)

    def body(c, rows, carry):
        m, lsum, acc = carry
        rb = rows.astype(BF16)
        big = _dot_t(lhs_ref[...], rb)
        kt = big[DEC_ROWS:DEC_ROWS + n_k].reshape(MLA_HEADS, MLA_NOPE, CHUNK_ROWS)
        kr = big[DEC_ROWS + n_k:]
        n2 = jnp.sum(kt * kt, axis=1) + jnp.sum(kr * kr, axis=0, keepdims=True)
        s = big[0:MLA_HEADS] * lax.rsqrt(n2 * inv_d + EPS)
        m_new = jnp.maximum(m, jnp.max(s, axis=-1, keepdims=True))
        alpha = jnp.exp(m - m_new)
        p = jnp.exp(s - m_new)
        lsum = alpha * lsum + jnp.sum(p, axis=-1, keepdims=True)
        acc = alpha * acc + _dot(p.astype(BF16), rb[:, 0:MLA_KV_RANK])
        return m_new, lsum, acc

    init = (jnp.full((MLA_HEADS, 1), NEG, F32), jnp.zeros((MLA_HEADS, 1), F32),
            jnp.zeros((MLA_HEADS, MLA_KV_RANK), F32))
    m, lsum, acc = _stream_pages(pt_ref, cache_ref, l, buf_ref, sem_ref, n_chunks, False, body, init)
    lane = lax.broadcasted_iota(jnp.int32, (MLA_HEADS, LANES), 1)
    acc_ref[0] = acc
    ml_ref[0] = jnp.where(lane == 0, m, jnp.where(lane == 1, lsum, 0.0))


def _mla_combine_kernel(acc_ref, ml_ref, q_ref, k_ref, v_ref, wuv_ref, o_ref):
    for h in range(MLA_HEADS):
        lanes = slice(h * LANES, (h + 1) * LANES)
        s_new = jnp.sum(q_ref[:, lanes].astype(F32) * k_ref[:, lanes].astype(F32), axis=-1, keepdims=True)
        m, lsum = ml_ref[h][:, 0:1], ml_ref[h][:, 1:2]
        mt = jnp.maximum(m, s_new)
        w_past, w_new = jnp.exp(m - mt), jnp.exp(s_new - mt)
        a = acc_ref[h]
        hi = a.astype(BF16)
        lo = (a - hi.astype(F32)).astype(BF16)
        up = _dot(hi, wuv_ref[h]) + _dot(lo, wuv_ref[h])
        v_new = v_ref[:, h * MLA_V:(h + 1) * MLA_V].astype(F32)
        o_ref[:, h * MLA_V:(h + 1) * MLA_V] = (w_past * up + w_new * v_new) / (w_past * lsum + w_new)


def _sb_decode_kernel(pt_ref, q_ref, tri_ref, cache_ref, o_ref, buf_ref, sem_ref, *, l, n_chunks):
    q = q_ref[0]
    tri2 = tri_ref[...]
    groups = CHUNK_ROWS // LANES

    def body(c, rows, carry):
        run, acc = carry
        kb = rows[:, 0:LANES].astype(BF16)
        vb = rows[:, LANES:2 * LANES].astype(BF16)
        z = _dot_t(q, kb)[0:SB_HEADS]
        sp = _softplus(z)
        st = jnp.concatenate([sp[:, g * LANES:(g + 1) * LANES] for g in range(groups)], axis=0)
        hi = st.astype(BF16)
        lo = (st - hi.astype(F32)).astype(BF16)
        cum2 = _dot(hi, tri2) + _dot(lo, tri2)
        parts = [None] * groups
        for g in reversed(range(groups)):
            loc = cum2[g * SB_HEADS:(g + 1) * SB_HEADS]
            parts[g] = jnp.exp(z[:, g * LANES:(g + 1) * LANES] - loc[:, 0:LANES] - run)
            run = run + loc[:, LANES:]
        a = jnp.concatenate(parts, axis=1).astype(BF16)
        return run, acc + _dot(a, vb)

    init = (jnp.zeros((SB_HEADS, LANES), F32), jnp.zeros((SB_HEADS, LANES), F32))
    _, acc = _stream_pages(pt_ref, cache_ref, l, buf_ref, sem_ref, n_chunks, True, body, init)
    first_group = lax.broadcasted_iota(jnp.int32, acc.shape, 0) < SB_HEADS // SB_KV
    o_ref[0] = jnp.where(first_group, acc, pltpu.roll(acc, HEAD_DIM, 1))


def _pad_rows(x, rows):
    return jnp.pad(x, ((0, 0), (0, rows - x.shape[1]), (0, 0)))


def sample_attention(r, l, cache_cmp, cache_slc, win_state, cache_mla, cache_sb, page_table,
                     phi_k, phi_v, nsa_k_norm, w_ukv, mla_k_norm):
    nseq, n_pages = page_table.shape
    n_chunks = n_pages // PAGES_PER_CHUNK
    nb_past = n_pages * PAGE_SIZE // CMP_BLOCK
    depth, n_pool = cache_cmp.shape[:2]
    paged = lambda c: c.reshape(depth, n_pool, PAGE_SIZE, -1)
    bufs = lambda rows, w: [pltpu.VMEM((2, rows, w), F32)]
    heads_f32 = lambda shape=(nseq, NSA_HEADS, LANES): jax.ShapeDtypeStruct(shape, F32)
    q8 = r["nq"][0].reshape(nseq, NSA_HEADS, HEAD_DIM)
    q8 = jnp.pad(q8, ((0, 0), (0, DEC_ROWS - NSA_HEADS), (0, LANES - HEAD_DIM)))
    new_cmp = _pad_rows(r["cmp"][0][:, None, :], 8)
    z64 = jnp.zeros((HEAD_DIM, HEAD_DIM), F32)
    wblk = jnp.block([[phi_k, z64], [z64, phi_v]]).astype(BF16)
    nbp = -(-(nb_past + 8) // LANES) * LANES
    o_cmp, picks = _paged_call(
        functools.partial(_nsa_cmp_decode_kernel, l=l, n_chunks=n_chunks), "nsa_cmp_decode", page_table,
        (q8, new_cmp, wblk, jnp.tile(nsa_k_norm[0], 2).reshape(1, LANES)),
        [_seq_spec(q8.shape), _seq_spec(new_cmp.shape), _const_spec(wblk.shape), _const_spec((1, LANES))],
        paged(cache_cmp),
        [heads_f32(), jax.ShapeDtypeStruct((nseq, 1, LANES), jnp.int32)],
        [_seq_spec((nseq, NSA_HEADS, LANES)), _seq_spec((nseq, 1, LANES))],
        [pltpu.VMEM((nbp, LANES), F32)] + bufs(CHUNK_ROWS, LANES))
    picks = picks[:, 0, :SEL_TOP_K]
    new_sw = _pad_rows(jnp.stack([r["slc"][0], r["win"][0]], axis=1), 8)
    gates = r["gates"][0][:, :3 * NSA_HEADS].reshape(nseq, 3, NSA_HEADS).transpose(0, 2, 1)
    gates = jnp.pad(gates, ((0, 0), (0, 0), (0, LANES - 3)))
    win = win_state.reshape(nseq, -1, LANES)
    gains = jnp.tile(nsa_k_norm[1:3], (1, 2))
    o_nsa = _paged_call(
        functools.partial(_nsa_sel_decode_kernel, l=l, nb_past=nb_past), "nsa_sel_decode", page_table,
        (q8, new_sw, win, o_cmp, gates, gains),
        [_seq_spec(q8.shape), _seq_spec(new_sw.shape), _seq_spec(win.shape), _seq_spec(o_cmp.shape),
         _seq_spec(gates.shape), _const_spec(gains.shape)],
        paged(cache_slc), heads_f32(), _seq_spec((nseq, NSA_HEADS, LANES)),
        bufs(SEL_TOP_K * SEL_BLOCK, LANES), extra_prefetch=(picks,))
    o_nsa = o_nsa[:, :, HEAD_DIM:].reshape(nseq, NSA_HEADS * HEAD_DIM)
    d_qk = MLA_NOPE + MLA_ROPE
    d_row = MLA_KV_RANK + MLA_ROPE
    wukv = w_ukv.reshape(MLA_KV_RANK, MLA_HEADS, MLA_NOPE + MLA_V)
    wuk = wukv[:, :, :MLA_NOPE]
    wukt = jnp.pad(wuk.transpose(1, 2, 0), ((0, 0), (0, LANES - MLA_NOPE), (0, 0))).astype(BF16)
    gk = jnp.pad(mla_k_norm, (0, LANES - d_qk)).reshape(1, LANES)
    mq, mk, mv = r["mla_q"][0], r["mla_k"][0], r["mla_v"][0]
    qabs = pl.pallas_call(
        _mla_absorb_kernel,
        out_shape=jax.ShapeDtypeStruct((MLA_HEADS, nseq, d_row), BF16),
        name="mla_absorb",
    )(mq, gk, wukt)
    qabs = _pad_rows(qabs.transpose(1, 0, 2), DEC_ROWS)
    sel = jnp.zeros((MLA_ROPE, d_row), F32).at[jnp.arange(MLA_ROPE), MLA_KV_RANK + jnp.arange(MLA_ROPE)].set(1.0)
    wk_rows = jnp.concatenate([jnp.pad(wuk.reshape(MLA_KV_RANK, -1).T, ((0, 0), (0, MLA_ROPE))), sel],
                              axis=0).astype(BF16)
    acc, ml = _paged_call(
        functools.partial(_mla_decode_kernel, l=l, n_chunks=n_chunks), "mla_decode", page_table,
        (qabs, wk_rows), [_seq_spec(qabs.shape), _const_spec(wk_rows.shape)],
        paged(cache_mla), [heads_f32(), heads_f32()], [_seq_spec((nseq, MLA_HEADS, LANES))] * 2,
        [pltpu.VMEM((DEC_ROWS + wk_rows.shape[0], d_row), BF16)] + bufs(CHUNK_ROWS, d_row))
    wuv = wukv[:, :, MLA_NOPE:].transpose(1, 0, 2).astype(BF16)
    o_mla = pl.pallas_call(
        _mla_combine_kernel,
        out_shape=jax.ShapeDtypeStruct((nseq, MLA_HEADS * MLA_V), F32),
        name="mla_combine",
    )(acc.transpose(1, 0, 2), ml.transpose(1, 0, 2), mq, mk, mv, wuv)
    g = SB_HEADS // SB_KV
    sq = r["sb_q"][0].reshape(nseq, SB_HEADS, HEAD_DIM)
    sq = jnp.concatenate([jnp.pad(sq[:, :g], ((0, 0), (0, 0), (0, HEAD_DIM))),
                          jnp.pad(sq[:, g:], ((0, 0), (0, 0), (HEAD_DIM, 0)))], axis=1)
    sq = _pad_rows(sq, DEC_ROWS)
    tri2 = jnp.asarray(np.concatenate([np.tril(np.ones((LANES, LANES), np.float32)),
                                       np.ones((LANES, LANES), np.float32)], axis=1), BF16)
    o_sb = _paged_call(
        functools.partial(_sb_decode_kernel, l=l, n_chunks=n_chunks), "sb_decode", page_table,
        (sq, tri2), [_seq_spec(sq.shape), _const_spec(tri2.shape)],
        paged(cache_sb), heads_f32((nseq, SB_HEADS, LANES)), _seq_spec((nseq, SB_HEADS, LANES)),
        bufs(CHUNK_ROWS, 2 * LANES))
    return o_nsa, o_mla, o_sb[:, :, :HEAD_DIM].reshape(nseq, SB_HEADS * HEAD_DIM)


def sample_layer(y, c, ada_w, ada_b, lw, l, cache_cmp, cache_slc, win_state, cache_mla, cache_sb,
                 page_table, phi_k, phi_v, nsa_k_norm, mla_w_ukv, mla_k_norm, rope_tab):
    nb_ = y.shape[0]
    flat = lambda a: a.reshape(1, nb_, D_MODEL)
    sh1, sc1, g1, sh2, sc2, g2 = (flat(a) for a in _ada(c, ada_w, ada_b))
    ys = flat(y)
    r = in_proj(ys, sh1, sc1, lw, rope_tab, nb_)
    o_nsa, o_mla, o_sb = sample_attention(r, l, cache_cmp, cache_slc, win_state, cache_mla, cache_sb,
                                          page_table, phi_k, phi_v, nsa_k_norm, mla_w_ukv, mla_k_norm)
    ys = merge_block(ys, sh1, sc1, g1, lw, o_nsa[None], o_mla[None], o_sb[None], nb_)
    ys = ffn_block(ys, sh2, sc2, g2, lw, nb_)
    rows5 = lambda a: a.reshape(nb_, -1, 2, 1, HEAD_DIM)
    win_new = jnp.concatenate([win_state.reshape(nb_, -1, 2 * HEAD_DIM)[:, 1:], r["win"][0][:, None, :]], axis=1)
    caches = (rows5(r["cmp"][0]), rows5(r["slc"][0]), rows5(win_new),
              r["mla_rows"][0][:, None, :], r["sb_rows"][0].reshape(nb_, 1, 2, SB_KV, HEAD_DIM))
    return ys.reshape(y.shape), caches


def kernel(x_prompt, x_sample, cache_nsa_cmp, cache_nsa_slc, state_nsa_win, cache_mla, cache_sb,
           page_table, c_prompt, c_sample, ada_w, ada_b, norm_mix, norm_ffn, w_in,
           nsa_q_norm, nsa_k_norm, nsa_phi_k, nsa_phi_v, mla_qa_norm, mla_kva_norm,
           mla_w_uq, mla_w_ukv, mla_q_norm, mla_k_norm, w_branch, w_out, ffn_w_up, ffn_w_down):
    S = x_prompt.shape[1]
    nb_, T = x_sample.shape[:2]
    assert T == 1, "the sample group decodes one token per sequence"
    rope_p = _rope_tables(jnp.arange(S, dtype=jnp.int32))
    rope_s = _rope_tables(jnp.full((nb_,), page_table.shape[1] * PAGE_SIZE, jnp.int32))
    y_p, y_s = x_prompt, x_sample
    outs_p, outs_s = [], []
    for l in range(DEPTH):
        lw = _layer_weights(l, norm_mix, norm_ffn, w_in, nsa_q_norm, nsa_k_norm, mla_qa_norm,
                            mla_kva_norm, mla_w_uq, mla_w_ukv, mla_q_norm, mla_k_norm, w_branch,
                            w_out, ffn_w_up, ffn_w_down)
        y_p, cp = prompt_layer(y_p, c_prompt, ada_w[l], ada_b[l], lw, nsa_phi_k[l], nsa_phi_v[l],
                               nsa_k_norm[l], rope_p)
        y_s, cs = sample_layer(y_s, c_sample, ada_w[l], ada_b[l], lw, l, cache_nsa_cmp, cache_nsa_slc,
                               state_nsa_win[l], cache_mla, cache_sb, page_table, nsa_phi_k[l],
                               nsa_phi_v[l], nsa_k_norm[l], mla_w_ukv[l], mla_k_norm[l], rope_s)
        outs_p.append(cp)
        outs_s.append(cs)
    stk = lambda outs, i: jnp.stack([o[i] for o in outs])
    res = [y_p, y_s]
    for i in range(5):
        res += [stk(outs_p, i), stk(outs_s, i)]
    return tuple(res)
```
